```python
import math
import jax, jax.numpy as jnp
from jax import lax
import numpy as np

D_MODEL = 2048
BATCH = 2
SEQ = 8192
DEPTH = 1

D_MIX = D_MODEL
D_ATTN = D_MIX // 2
D_SSM = D_MIX - D_ATTN
QK_HEAD_DIM = 64
V_HEAD_DIM = 2 * QK_HEAD_DIM
N_ATTN_HEADS = D_ATTN // V_HEAD_DIM
D_QK = 2 * N_ATTN_HEADS * QK_HEAD_DIM
SSM_GROUP = 16
N_SSM_GROUPS = D_SSM // SSM_GROUP
SSM_STATE = 64
D_IN = 2 * D_QK + D_ATTN + D_SSM
D_FF = ((8 * D_MODEL // 3) + 255) // 256 * 256
CONV_WIDTH = 3
N_BUCKETS = 32
MAX_DISTANCE = 128
Q_BLOCK = 128
EPS = 1e-6
DT_MIN = 1e-3
DT_MAX = 1e-1

kernel_name = "hymba_style_diffattn_s5_convffn_layer"


def rms_norm(x, g):
    xf = x.astype(jnp.float32)
    y = xf * lax.rsqrt(jnp.mean(xf * xf, axis=-1, keepdims=True) + EPS)
    return (y * g.astype(jnp.float32)).astype(x.dtype)


def t5_causal_bucket(dist):
    n = jnp.maximum(dist, 0)
    max_exact = N_BUCKETS // 2
    nf = jnp.maximum(n, 1).astype(jnp.float32)
    large = max_exact + (jnp.log(nf / max_exact) / math.log(MAX_DISTANCE / max_exact)
                         * (N_BUCKETS - max_exact)).astype(jnp.int32)
    large = jnp.minimum(large, N_BUCKETS - 1)
    return jnp.where(n < max_exact, n, large)


def diff_attention(q, k, v, rel_bias, lam_q1, lam_k1, lam_q2, lam_k2, sub_norm, lambda_init):
    B, S = q.shape[0], q.shape[1]
    H = N_ATTN_HEADS
    nb = S // Q_BLOCK
    lam = (jnp.exp(jnp.sum(lam_q1.astype(jnp.float32) * lam_k1.astype(jnp.float32)))
           - jnp.exp(jnp.sum(lam_q2.astype(jnp.float32) * lam_k2.astype(jnp.float32)))
           + lambda_init)
    k_pos = jnp.arange(S, dtype=jnp.int32)
    q_blocks = (q * (QK_HEAD_DIM ** -0.5)).reshape(B, nb, Q_BLOCK, 2 * H, QK_HEAD_DIM)
    q_blocks = q_blocks.transpose(1, 0, 2, 3, 4)

    def one_block(args):
        qb, i = args
        q_pos = i * Q_BLOCK + jnp.arange(Q_BLOCK, dtype=jnp.int32)
        dist = q_pos[:, None] - k_pos[None, :]
        bias = jnp.transpose(rel_bias[t5_causal_bucket(dist)], (2, 0, 1))
        s = jnp.einsum('bqhd,bkhd->bhqk', qb, k).astype(jnp.float32)
        s = s.reshape(B, H, 2, Q_BLOCK, S) + bias[None, :, None].astype(jnp.float32)
        s = jnp.where(dist >= 0, s, -jnp.inf)
        p = jax.nn.softmax(s, axis=-1)
        a = p[:, :, 0] - lam * p[:, :, 1]
        return jnp.einsum('bhqk,bkhd->bqhd', a.astype(v.dtype), v)

    out = lax.map(one_block, (q_blocks, jnp.arange(nb, dtype=jnp.int32)))
    out = out.transpose(1, 0, 2, 3, 4).reshape(B, S, H, V_HEAD_DIM)
    out = rms_norm(out, sub_norm) * (1.0 - lambda_init)
    return out.reshape(B, S, H * V_HEAD_DIM)


def _complex_affine_combine(e_i, e_j):
    ar_i, ai_i, br_i, bi_i = e_i
    ar_j, ai_j, br_j, bi_j = e_j
    ar = ar_j * ar_i - ai_j * ai_i
    ai = ar_j * ai_i + ai_j * ar_i
    br = ar_j * br_i - ai_j * bi_i + br_j
    bi = ar_j * bi_i + ai_j * br_i + bi_j
    return (ar, ai, br, bi)


def s5_mixer(u, a_re, a_im, log_dt, b_re, b_im, c_re, c_im, d_skip, w_glu):
    B, S = u.shape[0], u.shape[1]
    G, Hc, P = N_SSM_GROUPS, SSM_GROUP, SSM_STATE
    uf = u.astype(jnp.float32).reshape(B, S, G, Hc)
    dt = jnp.exp(log_dt.astype(jnp.float32))[:, None]
    ar = a_re.astype(jnp.float32)
    ai = a_im.astype(jnp.float32)
    mag = jnp.exp(ar * dt)
    lb_re = mag * jnp.cos(ai * dt)
    lb_im = mag * jnp.sin(ai * dt)
    den = ar * ar + ai * ai
    nr, ni = lb_re - 1.0, lb_im
    f_re = (nr * ar + ni * ai) / den
    f_im = (ni * ar - nr * ai) / den
    br = b_re.astype(jnp.float32)
    bi = b_im.astype(jnp.float32)
    bb_re = f_re[..., None] * br - f_im[..., None] * bi
    bb_im = f_re[..., None] * bi + f_im[..., None] * br
    bu_re = jnp.einsum('bsgh,gph->bsgp', uf, bb_re)
    bu_im = jnp.einsum('bsgh,gph->bsgp', uf, bb_im)
    a_seq_re = jnp.broadcast_to(lb_re, (1, S, G, P))
    a_seq_im = jnp.broadcast_to(lb_im, (1, S, G, P))
    _, _, x_re, x_im = lax.associative_scan(
        _complex_affine_combine, (a_seq_re, a_seq_im, bu_re, bu_im), axis=1)
    y = (jnp.einsum('bsgp,ghp->bsgh', x_re, c_re.astype(jnp.float32))
         - jnp.einsum('bsgp,ghp->bsgh', x_im, c_im.astype(jnp.float32))
         + d_skip.astype(jnp.float32) * uf)
    y = jax.nn.gelu(y.reshape(B, S, D_SSM)).astype(u.dtype)
    z = y @ w_glu
    return z[..., :D_SSM] * jax.nn.sigmoid(z[..., D_SSM:])


def conv_ffn(x, w_up, conv_w, conv_b, w_down):
    h = x @ w_up
    h = lax.conv_general_dilated(h, conv_w.astype(h.dtype), window_strides=(1,),
                                 padding=[(CONV_WIDTH - 1, 0)],
                                 dimension_numbers=('NWC', 'WIO', 'NWC'),
                                 feature_group_count=2 * D_FF) + conv_b
    g, val = jnp.split(h, 2, axis=-1)
    return (jax.nn.silu(g) * val) @ w_down


def setup_inputs(seed: int = 0) -> dict:
    key = jax.random.key(seed)
    ks = jax.random.split(key, 26)
    nrm = lambda k, shape, s: jax.random.normal(k, shape, jnp.float32) * s
    L, G, P, Hc = DEPTH, N_SSM_GROUPS, SSM_STATE, SSM_GROUP
    n_idx = jnp.arange(P, dtype=jnp.float32)
    return {
        "x": nrm(ks[0], (BATCH, SEQ, D_MODEL), 1.0),
        "ln_pre_mix": 1.0 + nrm(ks[1], (L, D_MODEL), 0.02),
        "ln_post_mix": 1.0 + nrm(ks[2], (L, D_MODEL), 0.02),
        "ln_pre_ffn": 1.0 + nrm(ks[3], (L, D_MODEL), 0.02),
        "ln_post_ffn": 1.0 + nrm(ks[4], (L, D_MODEL), 0.02),
        "w_in": nrm(ks[5], (L, D_MODEL, D_IN), D_MODEL ** -0.5),
        "lam_q1": nrm(ks[6], (L, QK_HEAD_DIM), 0.1),
        "lam_k1": nrm(ks[7], (L, QK_HEAD_DIM), 0.1),
        "lam_q2": nrm(ks[8], (L, QK_HEAD_DIM), 0.1),
        "lam_k2": nrm(ks[9], (L, QK_HEAD_DIM), 0.1),
        "attn_sub_norm": 1.0 + nrm(ks[10], (L, V_HEAD_DIM), 0.02),
        "rel_bias": nrm(ks[11], (N_BUCKETS, N_ATTN_HEADS), 0.5),
        "ssm_a_re": -0.5 + nrm(ks[12], (L, G, P), 0.01),
        "ssm_a_im": math.pi * n_idx + nrm(ks[13], (L, G, P), 0.01),
        "ssm_log_dt": jax.random.uniform(ks[14], (L, G), jnp.float32,
                                         math.log(DT_MIN), math.log(DT_MAX)),
        "ssm_b_re": nrm(ks[15], (L, G, P, Hc), (2 * Hc) ** -0.5),
        "ssm_b_im": nrm(ks[16], (L, G, P, Hc), (2 * Hc) ** -0.5),
        "ssm_c_re": nrm(ks[17], (L, G, Hc, P), (2 * P) ** -0.5),
        "ssm_c_im": nrm(ks[18], (L, G, Hc, P), (2 * P) ** -0.5),
        "ssm_d": nrm(ks[19], (L, G, Hc), 1.0),
        "ssm_w_glu": nrm(ks[20], (L, D_SSM, 2 * D_SSM), D_SSM ** -0.5),
        "w_out": nrm(ks[21], (L, D_MIX, D_MODEL), D_MIX ** -0.5),
        "w_up": nrm(ks[22], (L, D_MODEL, 2 * D_FF), D_MODEL ** -0.5),
        "conv_w": nrm(ks[23], (L, CONV_WIDTH, 1, 2 * D_FF), CONV_WIDTH ** -0.5),
        "conv_b": nrm(ks[24], (L, 2 * D_FF), 0.02),
        "w_down": nrm(ks[25], (L, D_FF, D_MODEL), D_FF ** -0.5),
    }


def reference(x, ln_pre_mix, ln_post_mix, ln_pre_ffn, ln_post_ffn, w_in,
              lam_q1, lam_k1, lam_q2, lam_k2, attn_sub_norm, rel_bias,
              ssm_a_re, ssm_a_im, ssm_log_dt, ssm_b_re, ssm_b_im, ssm_c_re, ssm_c_im,
              ssm_d, ssm_w_glu, w_out, w_up, conv_w, conv_b, w_down):
    B, S, _ = x.shape
    H = N_ATTN_HEADS
    for l in range(DEPTH):
        lambda_init = 0.8 - 0.6 * math.exp(-0.3 * l)
        xn = rms_norm(x, ln_pre_mix[l])
        h = xn @ w_in[l]
        q = h[..., :D_QK].reshape(B, S, 2 * H, QK_HEAD_DIM)
        k = h[..., D_QK:2 * D_QK].reshape(B, S, 2 * H, QK_HEAD_DIM)
        v = h[..., 2 * D_QK:2 * D_QK + D_ATTN].reshape(B, S, H, V_HEAD_DIM)
        u = h[..., 2 * D_QK + D_ATTN:]
        attn_out = diff_attention(q, k, v, rel_bias, lam_q1[l], lam_k1[l], lam_q2[l],
                                  lam_k2[l], attn_sub_norm[l], lambda_init)
        ssm_out = s5_mixer(u, ssm_a_re[l], ssm_a_im[l], ssm_log_dt[l], ssm_b_re[l],
                           ssm_b_im[l], ssm_c_re[l], ssm_c_im[l], ssm_d[l], ssm_w_glu[l])
        mix = jnp.concatenate([attn_out, ssm_out.astype(attn_out.dtype)], axis=-1) @ w_out[l]
        x = x + rms_norm(mix, ln_post_mix[l])
        f = conv_ffn(rms_norm(x, ln_pre_ffn[l]), w_up[l], conv_w[l], conv_b[l], w_down[l])
        x = x + rms_norm(f, ln_post_ffn[l])
    return x
```

```python
import functools
import math

import numpy as np
import jax
import jax.numpy as jnp
from jax import lax
from jax.experimental import pallas as pl
from jax.experimental.pallas import tpu as pltpu

D_MODEL = 2048
D_ATTN = 1024
D_SSM = 1024
QK_HEAD_DIM = 64
V_HEAD_DIM = 128
N_ATTN_HEADS = 8
D_QK = 1024
SSM_GROUP = 16
N_SSM_GROUPS = 64
SSM_STATE = 64
D_IN = 4096
D_FF = 5632
N_BUCKETS = 32
MAX_DISTANCE = 128
EPS = 1e-6
LAMBDA_INIT = 0.8 - 0.6 * math.exp(-0.3 * 0)

N_CPLX = N_SSM_GROUPS * SSM_STATE
GROUPS_PER_OCTET = 8
N_OCTETS = N_SSM_GROUPS // GROUPS_PER_OCTET
OCTET_STATES = GROUPS_PER_OCTET * SSM_STATE

ATTN_TILE = 256
MASK_VALUE = -1e30

VMEM_LIMIT = 56 * 1024 * 1024

F32 = jnp.float32
BF16 = jnp.bfloat16


def _params(sem):
    return pltpu.CompilerParams(dimension_semantics=sem, vmem_limit_bytes=VMEM_LIMIT)


def _norm_proj_kernel(x_ref, g_ref, w_ref, o_ref, xn_ref):
    @pl.when(pl.program_id(1) == 0)
    def _():
        x = x_ref[...]
        y = x * lax.rsqrt(jnp.mean(x * x, axis=-1, keepdims=True) + EPS)
        xn_ref[...] = (y * g_ref[...]).astype(BF16)

    o_ref[...] = jnp.dot(xn_ref[...], w_ref[...], preferred_element_type=F32).astype(o_ref.dtype)


def _norm_proj(x2, g, w, tm=512, tn=1024):
    m, d = x2.shape
    n = w.shape[1]
    return pl.pallas_call(
        _norm_proj_kernel,
        grid=(m // tm, n // tn),
        in_specs=[
            pl.BlockSpec((tm, d), lambda i, j: (i, 0)),
            pl.BlockSpec((1, d), lambda i, j: (0, 0)),
            pl.BlockSpec((d, tn), lambda i, j: (0, j)),
        ],
        out_specs=pl.BlockSpec((tm, tn), lambda i, j: (i, j)),
        out_shape=jax.ShapeDtypeStruct((m, n), BF16),
        scratch_shapes=[pltpu.VMEM((tm, d), BF16)],
        compiler_params=_params(("parallel", "arbitrary")),
        name="norm_in_proj",
    )(x2, g, w)


def _attn_kernel(lam_ref, q_ref, k_ref, v_ref, bias_ref, sub_ref, o_ref, acc_ref, m_ref):
    t = ATTN_TILE
    qi = pl.program_id(2)

    q = q_ref[...]
    lane = lax.broadcasted_iota(jnp.int32, q.shape, 1)
    zero = jnp.zeros_like(q)
    scale = jnp.asarray(QK_HEAD_DIM ** -0.5, q.dtype)
    q2 = jnp.concatenate([jnp.where(lane < QK_HEAD_DIM, q, zero),
                          jnp.where(lane >= QK_HEAD_DIM, q, zero)], axis=0) * scale

    ones = jnp.ones((t, V_HEAD_DIM), BF16)

    def scores(kt):
        k = k_ref[pl.ds(pl.multiple_of(kt * t, t), t), :]
        return lax.dot_general(q2, k, (((1,), (1,)), ((), ())), preferred_element_type=F32)

    def v_ext(kt):
        v = v_ref[pl.ds(pl.multiple_of(kt * t, t), t), :]
        return jnp.concatenate([v, ones], axis=1)

    def update(s, kt):
        m_old = m_ref[...]
        m_new = jnp.maximum(m_old, jnp.max(s, axis=-1, keepdims=True))
        alpha = jnp.exp(m_old - m_new)
        p = jnp.exp(s - m_new).astype(BF16)
        acc_ref[...] = alpha * acc_ref[...] + jnp.dot(p, v_ext(kt), preferred_element_type=F32)
        m_ref[...] = m_new

    bias0 = bias_ref[0]
    s = scores(qi) + jnp.concatenate([bias0, bias0], axis=0)
    m0 = jnp.max(s, axis=-1, keepdims=True)
    p0 = jnp.exp(s - m0).astype(BF16)
    acc_ref[...] = jnp.dot(p0, v_ext(qi), preferred_element_type=F32)
    m_ref[...] = m0

    @pl.when(qi >= 1)
    def _():
        bias1 = bias_ref[1]
        update(scores(qi - 1) + jnp.concatenate([bias1, bias1], axis=0), qi - 1)

    def far_body(kt, carry):
        update(scores(kt), kt)
        return carry

    lax.fori_loop(0, jnp.maximum(qi - 1, 0), far_body, 0)

    lam_v = lam_ref[...]
    lam = (jnp.exp(jnp.sum(lam_v[0:1] * lam_v[1:2], axis=-1, keepdims=True))
           - jnp.exp(jnp.sum(lam_v[2:3] * lam_v[3:4], axis=-1, keepdims=True))
           + LAMBDA_INIT)
    acc = acc_ref[...]
    out_a = acc[:t, :V_HEAD_DIM] / acc[:t, V_HEAD_DIM:V_HEAD_DIM + 1]
    out_b = acc[t:, :V_HEAD_DIM] / acc[t:, V_HEAD_DIM:V_HEAD_DIM + 1]
    o = out_a - lam * out_b
    y = o * lax.rsqrt(jnp.mean(o * o, axis=-1, keepdims=True) + EPS)
    o_ref[...] = ((y * sub_ref[...]) * (1.0 - LAMBDA_INIT)).astype(o_ref.dtype)


def _diff_attention(h3, lam_vecs, bias_tiles, sub_norm):
    b, s, _ = h3.shape
    t = ATTN_TILE
    nh = N_ATTN_HEADS
    return pl.pallas_call(
        _attn_kernel,
        grid=(b, nh, s // t),
        in_specs=[
            pl.BlockSpec((4, QK_HEAD_DIM), lambda bi, hi, qi: (0, 0)),
            pl.BlockSpec((None, t, 128), lambda bi, hi, qi: (bi, qi, hi)),
            pl.BlockSpec((None, s, 128), lambda bi, hi, qi: (bi, 0, nh + hi)),
            pl.BlockSpec((None, s, 128), lambda bi, hi, qi: (bi, 0, 2 * nh + hi)),
            pl.BlockSpec((None, 2, t, t), lambda bi, hi, qi: (hi, 0, 0, 0)),
            pl.BlockSpec((1, V_HEAD_DIM), lambda bi, hi, qi: (0, 0)),
        ],
        out_specs=pl.BlockSpec((None, t, 128), lambda bi, hi, qi: (bi, qi, hi)),
        out_shape=jax.ShapeDtypeStruct((b, s, D_ATTN), BF16),
        scratch_shapes=[pltpu.VMEM((2 * t, 2 * V_HEAD_DIM), F32),
                        pltpu.VMEM((2 * t, 1), F32)],
        compiler_params=_params(("parallel", "parallel", "arbitrary")),
        name="diff_attention",
    )(lam_vecs, h3, h3, h3, bias_tiles, sub_norm)


def _attention_bias_tiles(rel_bias, s):
    t = ATTN_TILE
    n = jnp.arange(2 * t, dtype=jnp.int32)
    max_exact = N_BUCKETS // 2
    nf = jnp.maximum(n, 1).astype(F32)
    large = max_exact + (jnp.log(nf / max_exact) / math.log(MAX_DISTANCE / max_exact)
                         * (N_BUCKETS - max_exact)).astype(jnp.int32)
    large = jnp.minimum(large, N_BUCKETS - 1)
    bucket = jnp.where(n < max_exact, n, large)
    far = np.arange(t + 1, s, dtype=np.float64)
    far_bucket = max_exact + np.floor(np.log(far / max_exact) / math.log(MAX_DISTANCE / max_exact)
                                      * (N_BUCKETS - max_exact) - 1e-6)
    assert far_bucket.min() >= N_BUCKETS - 1
    table = rel_bias[bucket] - rel_bias[N_BUCKETS - 1][None, :]
    i = np.arange(t)[:, None]
    j = np.arange(t)[None, :]
    d0 = i - j
    d1 = i - j + t
    tile0 = jnp.where(jnp.asarray(d0 >= 0)[:, :, None], table[np.maximum(d0, 0)], MASK_VALUE)
    tile1 = table[d1]
    tiles = jnp.stack([tile0, tile1], axis=0)
    return jnp.transpose(tiles, (3, 0, 1, 2)).astype(F32)


SCAN_ROWS = 8
SCAN_COLS = 512


def _s5_kernel(u_ref, bre_ref, bim_ref, cre_ref, cim_ref, coef_ref, d_ref, wglu_ref,
               o_ref, xr_ref, xi_ref, carry_ref):
    tt = u_ref.shape[0]

    @pl.when(pl.program_id(1) == 0)
    def _():
        carry_ref[...] = jnp.zeros_like(carry_ref)

    u = u_ref[...]
    for j in range(N_OCTETS):
        uj = u[:, j * 128:(j + 1) * 128]
        cols = slice(j * OCTET_STATES, (j + 1) * OCTET_STATES)
        xr_ref[:, cols] = jnp.dot(uj, bre_ref[j], preferred_element_type=F32)
        xi_ref[:, cols] = jnp.dot(uj, bim_ref[j], preferred_element_type=F32)

    for c in range(N_CPLX // SCAN_COLS):
        cols = slice(c * SCAN_COLS, (c + 1) * SCAN_COLS)
        coef = [coef_ref[i, :, cols] for i in range(8)]

        def body(r, carry, cols=cols, coef=coef):
            cr, ci = carry
            rows = pl.ds(pl.multiple_of(r * SCAN_ROWS, SCAN_ROWS), SCAN_ROWS)
            xr = xr_ref[rows, cols]
            xi = xi_ref[rows, cols]
            for idx, k in enumerate((1, 2, 4)):
                ar, ai = coef[2 * idx], coef[2 * idx + 1]
                sr = pltpu.roll(xr, k, 0)
                si = pltpu.roll(xi, k, 0)
                xr, xi = xr + ar * sr - ai * si, xi + ar * si + ai * sr
            pr, pi = coef[6], coef[7]
            xr, xi = xr + pr * cr - pi * ci, xi + pr * ci + pi * cr
            xr_ref[rows, cols] = xr
            xi_ref[rows, cols] = xi
            return xr[SCAN_ROWS - 1:SCAN_ROWS, :], xi[SCAN_ROWS - 1:SCAN_ROWS, :]

        cr, ci = lax.fori_loop(0, tt // SCAN_ROWS, body,
                               (carry_ref[0:1, cols], carry_ref[1:2, cols]))
        carry_ref[0:1, cols] = cr
        carry_ref[1:2, cols] = ci

    ys = []
    for j in range(N_OCTETS):
        cols = slice(j * OCTET_STATES, (j + 1) * OCTET_STATES)
        ys.append(jnp.dot(xr_ref[:, cols].astype(BF16), cre_ref[j], preferred_element_type=F32)
                  + jnp.dot(xi_ref[:, cols].astype(BF16), cim_ref[j], preferred_element_type=F32))
    y = jnp.concatenate(ys, axis=1) + d_ref[...] * u.astype(F32)
    y = jax.nn.gelu(y).astype(BF16)
    z = jnp.dot(y, wglu_ref[...], preferred_element_type=F32)
    o_ref[...] = (z[:, :D_SSM] * jax.nn.sigmoid(z[:, D_SSM:])).astype(o_ref.dtype)


def _s5_mixer(h3, bre, bim, cre, cim, coef, d_row, w_glu, tt=256):
    b, s, _ = h3.shape
    const3 = lambda bi, ti: (0, 0, 0)
    const2 = lambda bi, ti: (0, 0)
    return pl.pallas_call(
        _s5_kernel,
        grid=(b, s // tt),
        in_specs=[
            pl.BlockSpec((None, tt, D_SSM), lambda bi, ti: (bi, ti, 3)),
            pl.BlockSpec(bre.shape, const3),
            pl.BlockSpec(bim.shape, const3),
            pl.BlockSpec(cre.shape, const3),
            pl.BlockSpec(cim.shape, const3),
            pl.BlockSpec(coef.shape, const3),
            pl.BlockSpec(d_row.shape, const2),
            pl.BlockSpec(w_glu.shape, const2),
        ],
        out_specs=pl.BlockSpec((None, tt, D_SSM), lambda bi, ti: (bi, ti, 0)),
        out_shape=jax.ShapeDtypeStruct((b, s, D_SSM), BF16),
        scratch_shapes=[pltpu.VMEM((tt, N_CPLX), F32),
                        pltpu.VMEM((tt, N_CPLX), F32),
                        pltpu.VMEM((8, N_CPLX), F32)],
        compiler_params=_params(("parallel", "arbitrary")),
        name="s5_mixer",
    )(h3, bre, bim, cre, cim, coef, d_row, w_glu)


def _s5_parameters(a_re, a_im, log_dt, b_re, b_im, c_re, c_im):
    g, p, hc = N_SSM_GROUPS, SSM_STATE, SSM_GROUP
    dt = jnp.exp(log_dt.astype(F32))[:, None]
    ar = a_re.astype(F32)
    ai = a_im.astype(F32)
    mag = jnp.exp(ar * dt)
    lb_re = mag * jnp.cos(ai * dt)
    lb_im = mag * jnp.sin(ai * dt)
    den = ar * ar + ai * ai
    nr, ni = lb_re - 1.0, lb_im
    f_re = (nr * ar + ni * ai) / den
    f_im = (ni * ar - nr * ai) / den
    br = b_re.astype(F32)
    bi = b_im.astype(F32)
    bb_re = f_re[..., None] * br - f_im[..., None] * bi
    bb_im = f_re[..., None] * bi + f_im[..., None] * br

    eye = jnp.eye(GROUPS_PER_OCTET, dtype=F32)

    def b_layout(bb):
        bb4 = bb.reshape(N_OCTETS, GROUPS_PER_OCTET, p, hc)
        return jnp.einsum('jgph,kg->jkhgp', bb4, eye).reshape(
            N_OCTETS, GROUPS_PER_OCTET * hc, OCTET_STATES).astype(BF16)

    def c_layout(cc):
        c4 = cc.astype(F32).reshape(N_OCTETS, GROUPS_PER_OCTET, hc, p)
        return jnp.einsum('jghp,kg->jgpkh', c4, eye).reshape(
            N_OCTETS, OCTET_STATES, GROUPS_PER_OCTET * hc).astype(BF16)

    def cmul(xr, xi, yr, yi):
        return xr * yr - xi * yi, xr * yi + xi * yr

    l1 = (lb_re.reshape(-1), lb_im.reshape(-1))
    pows = [l1]
    for _ in range(SCAN_ROWS - 1):
        pows.append(cmul(*pows[-1], *l1))
    row = jnp.arange(SCAN_ROWS)[:, None]
    coefs = []
    for k in (1, 2, 4):
        keep = (row >= k).astype(F32)
        coefs += [keep * pows[k - 1][0][None, :], keep * pows[k - 1][1][None, :]]
    coefs += [jnp.stack([pw[0] for pw in pows]), jnp.stack([pw[1] for pw in pows])]
    coef = jnp.stack(coefs)
    return b_layout(bb_re), b_layout(bb_im), c_layout(c_re), -c_layout(c_im), coef


def _out_proj_kernel(a_ref, s_ref, w_ref, x_ref, gpost_ref, gpre_ref, x1_ref, xn_ref):
    mix = (jnp.dot(a_ref[...], w_ref[:D_ATTN, :], preferred_element_type=F32)
           + jnp.dot(s_ref[...], w_ref[D_ATTN:, :], preferred_element_type=F32))
    y = mix * lax.rsqrt(jnp.mean(mix * mix, axis=-1, keepdims=True) + EPS)
    x1 = x_ref[...] + y * gpost_ref[...]
    x1_ref[...] = x1
    z = x1 * lax.rsqrt(jnp.mean(x1 * x1, axis=-1, keepdims=True) + EPS)
    xn_ref[...] = (z * gpre_ref[...]).astype(xn_ref.dtype)


def _out_proj(attn2, ssm2, w_out, x2, g_post, g_pre, tm=256):
    m, d = x2.shape
    return pl.pallas_call(
        _out_proj_kernel,
        grid=(m // tm,),
        in_specs=[
            pl.BlockSpec((tm, D_ATTN), lambda i: (i, 0)),
            pl.BlockSpec((tm, D_SSM), lambda i: (i, 0)),
            pl.BlockSpec(w_out.shape, lambda i: (0, 0)),
            pl.BlockSpec((tm, d), lambda i: (i, 0)),
            pl.BlockSpec((1, d), lambda i: (0, 0)),
            pl.BlockSpec((1, d), lambda i: (0, 0)),
        ],
        out_specs=[pl.BlockSpec((tm, d), lambda i: (i, 0)),
                   pl.BlockSpec((tm, d), lambda i: (i, 0))],
        out_shape=[jax.ShapeDtypeStruct((m, d), F32),
                   jax.ShapeDtypeStruct((m, d), BF16)],
        compiler_params=_params(("parallel",)),
        name="out_proj_norm",
    )(attn2, ssm2, w_out, x2, g_post, g_pre)


CONV_HALO = 8


def _up_conv_kernel(x_ref, wg_ref, wv_ref, cwg_ref, cwv_ref, cbg_ref, cbv_ref, o_ref,
                    hg_ref, hv_ref, *, tiles_per_seq):
    tm = x_ref.shape[0]
    seq_start = (pl.program_id(1) % tiles_per_seq) == 0
    x = x_ref[...]

    def conv(h_ref, w_ref, cw_ref, cb_ref):
        h = jnp.dot(x, w_ref[...], preferred_element_type=F32)
        @pl.when(seq_start)
        def _():
            h_ref[0:CONV_HALO, :] = jnp.zeros((CONV_HALO, h_ref.shape[1]), F32)

        @pl.when(jnp.logical_not(seq_start))
        def _():
            h_ref[0:CONV_HALO, :] = h_ref[tm:tm + CONV_HALO, :]

        h_ref[CONV_HALO:, :] = h
        hm1 = h_ref[CONV_HALO - 1:CONV_HALO - 1 + tm, :]
        hm2 = h_ref[CONV_HALO - 2:CONV_HALO - 2 + tm, :]
        return cw_ref[0:1, :] * hm2 + cw_ref[1:2, :] * hm1 + cw_ref[2:3, :] * h + cb_ref[...]

    g = conv(hg_ref, wg_ref, cwg_ref, cbg_ref)
    v = conv(hv_ref, wv_ref, cwv_ref, cbv_ref)
    o_ref[...] = (jax.nn.silu(g) * v).astype(o_ref.dtype)


def _up_conv(xn2, w_up, conv_w, conv_b, seq_len, tm=512, tn=512):
    m, d = xn2.shape
    nj = D_FF // tn
    kern = functools.partial(_up_conv_kernel, tiles_per_seq=seq_len // tm)
    return pl.pallas_call(
        kern,
        grid=(nj, m // tm),
        in_specs=[
            pl.BlockSpec((tm, d), lambda j, i: (i, 0)),
            pl.BlockSpec((d, tn), lambda j, i: (0, j)),
            pl.BlockSpec((d, tn), lambda j, i: (0, nj + j)),
            pl.BlockSpec((3, tn), lambda j, i: (0, j)),
            pl.BlockSpec((3, tn), lambda j, i: (0, nj + j)),
            pl.BlockSpec((1, tn), lambda j, i: (0, j)),
            pl.BlockSpec((1, tn), lambda j, i: (0, nj + j)),
        ],
        out_specs=pl.BlockSpec((tm, tn), lambda j, i: (i, j)),
        out_shape=jax.ShapeDtypeStruct((m, D_FF), BF16),
        scratch_shapes=[pltpu.VMEM((tm + CONV_HALO, tn), F32),
                        pltpu.VMEM((tm + CONV_HALO, tn), F32)],
        compiler_params=_params(("parallel", "arbitrary")),
        name="up_conv_gate",
    )(xn2, w_up, w_up, conv_w, conv_w, conv_b, conv_b)


def _down_proj_kernel(a_ref, w_ref, x_ref, g_ref, o_ref, acc_ref):
    k = pl.program_id(1)

    @pl.when(k == 0)
    def _():
        acc_ref[...] = jnp.zeros_like(acc_ref)

    acc_ref[...] += jnp.dot(a_ref[...], w_ref[...], preferred_element_type=F32)

    @pl.when(k == pl.num_programs(1) - 1)
    def _():
        f = acc_ref[...]
        y = f * lax.rsqrt(jnp.mean(f * f, axis=-1, keepdims=True) + EPS)
        o_ref[...] = x_ref[...] + y * g_ref[...]


def _down_proj(gated, w_down, x1, g_post, tm=512, tk=512):
    m, kdim = gated.shape
    d = w_down.shape[1]
    return pl.pallas_call(
        _down_proj_kernel,
        grid=(m // tm, kdim // tk),
        in_specs=[
            pl.BlockSpec((tm, tk), lambda i, k: (i, k)),
            pl.BlockSpec((tk, d), lambda i, k: (k, 0)),
            pl.BlockSpec((tm, d), lambda i, k: (i, 0)),
            pl.BlockSpec((1, d), lambda i, k: (0, 0)),
        ],
        out_specs=pl.BlockSpec((tm, d), lambda i, k: (i, 0)),
        out_shape=jax.ShapeDtypeStruct((m, d), F32),
        scratch_shapes=[pltpu.VMEM((tm, d), F32)],
        compiler_params=_params(("parallel", "arbitrary")),
        name="down_proj_norm",
    )(gated, w_down, x1, g_post)


def kernel(x, ln_pre_mix, ln_post_mix, ln_pre_ffn, ln_post_ffn, w_in, lam_q1, lam_k1, lam_q2, lam_k2, attn_sub_norm, rel_bias, ssm_a_re, ssm_a_im, ssm_log_dt, ssm_b_re, ssm_b_im, ssm_c_re, ssm_c_im, ssm_d, ssm_w_glu, w_out, w_up, conv_w, conv_b, w_down):
    b, s, d = x.shape
    assert d == D_MODEL and s % ATTN_TILE == 0
    l = 0
    x2 = x.reshape(b * s, d)

    h = _norm_proj(x2, ln_pre_mix[l][None, :], w_in[l].astype(BF16))
    h3 = h.reshape(b, s, D_IN)

    lam_vecs = jnp.stack([lam_q1[l], lam_k1[l], lam_q2[l], lam_k2[l]]).astype(F32)
    bias_tiles = _attention_bias_tiles(rel_bias.astype(F32), s)
    attn = _diff_attention(h3, lam_vecs, bias_tiles, attn_sub_norm[l][None, :].astype(F32))

    bre, bim, cre, cim, coef = _s5_parameters(ssm_a_re[l], ssm_a_im[l], ssm_log_dt[l],
                                              ssm_b_re[l], ssm_b_im[l], ssm_c_re[l], ssm_c_im[l])
    ssm = _s5_mixer(h3, bre, bim, cre, cim, coef,
                    ssm_d[l].reshape(1, D_SSM).astype(F32), ssm_w_glu[l].astype(BF16))

    x1, xn2 = _out_proj(attn.reshape(b * s, D_ATTN), ssm.reshape(b * s, D_SSM),
                        w_out[l].astype(BF16), x2, ln_post_mix[l][None, :], ln_pre_ffn[l][None, :])

    gated = _up_conv(xn2, w_up[l].astype(BF16), conv_w[l].reshape(3, 2 * D_FF),
                     conv_b[l].reshape(1, 2 * D_FF), s)
    y = _down_proj(gated, w_down[l].astype(BF16), x1, ln_post_ffn[l][None, :])
    return y.reshape(b, s, d)
```

```python
import functools
import math

import numpy as np
import jax
import jax.numpy as jnp
from jax import lax
from jax.experimental import pallas as pl
from jax.experimental.pallas import tpu as pltpu

D_MODEL = 2048
D_ATTN = 1024
D_SSM = 1024
QK_HEAD_DIM = 64
V_HEAD_DIM = 128
N_ATTN_HEADS = 8
D_QK = 1024
SSM_GROUP = 16
N_SSM_GROUPS = 64
SSM_STATE = 64
D_IN = 4096
D_FF = 5632
N_BUCKETS = 32
MAX_DISTANCE = 128
EPS = 1e-6
LAMBDA_INIT = 0.8 - 0.6 * math.exp(-0.3 * 0)

N_CPLX = N_SSM_GROUPS * SSM_STATE
GROUPS_PER_OCTET = 8
N_OCTETS = N_SSM_GROUPS // GROUPS_PER_OCTET
OCTET_STATES = GROUPS_PER_OCTET * SSM_STATE

ATTN_Q_TILE = 512
ATTN_K_TILE = 256
FAR_DISTANCE = 2 * MAX_DISTANCE + 1
N_BIAS_TILES = -(-(FAR_DISTANCE + ATTN_Q_TILE - 1) // ATTN_K_TILE)
MASK_VALUE = -1e30

VMEM_LIMIT = 56 * 1024 * 1024

F32 = jnp.float32
BF16 = jnp.bfloat16


def _params(sem):
    return pltpu.CompilerParams(dimension_semantics=sem, vmem_limit_bytes=VMEM_LIMIT)


def _norm_proj_kernel(x_ref, g_ref, w_ref, o_ref, xn_ref):
    @pl.when(pl.program_id(1) == 0)
    def _():
        x = x_ref[...]
        y = x * lax.rsqrt(jnp.mean(x * x, axis=-1, keepdims=True) + EPS)
        xn_ref[...] = (y * g_ref[...]).astype(BF16)

    o_ref[...] = jnp.dot(xn_ref[...], w_ref[...], preferred_element_type=F32).astype(o_ref.dtype)


def _norm_proj(x2, g, w, tm=512, tn=1024):
    m, d = x2.shape
    n = w.shape[1]
    return pl.pallas_call(
        _norm_proj_kernel,
        grid=(m // tm, n // tn),
        in_specs=[
            pl.BlockSpec((tm, d), lambda i, j: (i, 0)),
            pl.BlockSpec((1, d), lambda i, j: (0, 0)),
            pl.BlockSpec((d, tn), lambda i, j: (0, j)),
        ],
        out_specs=pl.BlockSpec((tm, tn), lambda i, j: (i, j)),
        out_shape=jax.ShapeDtypeStruct((m, n), BF16),
        scratch_shapes=[pltpu.VMEM((tm, d), BF16)],
        compiler_params=_params(("parallel", "arbitrary")),
        name="norm_in_proj",
    )(x2, g, w)


def _attn_kernel(lam_ref, q_ref, k_ref, v_ref, bias_ref, sub_ref, o_ref,
                 acc_ref, m_ref, s_ref, p_ref, alpha_ref, mt_ref):
    t = ATTN_Q_TILE
    tk = ATTN_K_TILE
    qi = pl.program_id(2)
    last = (t // tk) * (qi + 1) - 1

    q = q_ref[...]
    lane = lax.broadcasted_iota(jnp.int32, q.shape, 1)
    zero = jnp.zeros_like(q)
    scale = jnp.asarray(QK_HEAD_DIM ** -0.5, q.dtype)
    q2 = jnp.concatenate([jnp.where(lane < QK_HEAD_DIM, q, zero),
                          jnp.where(lane >= QK_HEAD_DIM, q, zero)], axis=0) * scale

    ones = jnp.ones((tk, V_HEAD_DIM), BF16)

    def scores(kt):
        k = k_ref[pl.ds(pl.multiple_of(kt * tk, tk), tk), :]
        return lax.dot_general(q2, k, (((1,), (1,)), ((), ())), preferred_element_type=F32)

    def v_ext(kt):
        v = v_ref[pl.ds(pl.multiple_of(kt * tk, tk), tk), :]
        return jnp.concatenate([v, ones], axis=1)

    def row_max(s):
        m = jnp.max(jnp.maximum(s[:, :V_HEAD_DIM], s[:, V_HEAD_DIM:]), axis=-1, keepdims=True)
        return jnp.broadcast_to(m, (2 * t, V_HEAD_DIM))

    def store_scores(kt, slot, bias_idx=None):
        s = scores(kt)
        if bias_idx is not None:
            bias = bias_ref[bias_idx]
            s = s + jnp.concatenate([bias, bias], axis=0)
        s_ref[slot] = s
        mt_ref[slot] = row_max(s)

    def softmax_stage(slot):
        s = s_ref[slot]
        m_old = m_ref[...]
        m_new = jnp.maximum(m_old, mt_ref[slot])
        alpha_ref[slot] = jnp.exp(m_old - m_new)
        m_ref[...] = m_new
        p_ref[slot] = jnp.exp(s - jnp.concatenate([m_new, m_new], axis=1)).astype(BF16)

    def pv_stage(kt, slot):
        a = alpha_ref[slot]
        acc_ref[...] = (acc_ref[...] * jnp.concatenate([a, a], axis=1)
                        + jnp.dot(p_ref[slot], v_ext(kt), preferred_element_type=F32))

    def step(j, slot, bias_idx=None):
        pv_stage(jnp.maximum(j - 1, 0), 1 - slot)
        softmax_stage(slot)
        store_scores(j + 1, 1 - slot, bias_idx)

    m_ref[...] = jnp.full(m_ref.shape, MASK_VALUE, F32)
    acc_ref[...] = jnp.zeros(acc_ref.shape, F32)
    p_ref[1] = jnp.zeros(p_ref.shape[1:], BF16)
    alpha_ref[1] = jnp.ones(alpha_ref.shape[1:], F32)

    assert N_BIAS_TILES == 3 and t == 2 * tk

    @pl.when(qi == 0)
    def _():
        store_scores(0, 0, bias_idx=1)

    @pl.when(qi > 0)
    def _():
        store_scores(0, 0)

    def pair_body(i, carry):
        step(2 * i, 0)
        step(2 * i + 1, 1)
        return carry

    lax.fori_loop(0, jnp.maximum(qi - 1, 0), pair_body, 0)

    @pl.when(qi > 0)
    def _():
        step(last - 3, 0, bias_idx=2)
        step(last - 2, 1, bias_idx=1)

    step(last - 1, 0, bias_idx=0)
    pv_stage(last - 1, 0)
    softmax_stage(1)
    pv_stage(last, 1)

    lam_v = lam_ref[...]
    lam = (jnp.exp(jnp.sum(lam_v[0:1] * lam_v[1:2], axis=-1, keepdims=True))
           - jnp.exp(jnp.sum(lam_v[2:3] * lam_v[3:4], axis=-1, keepdims=True))
           + LAMBDA_INIT)
    acc = acc_ref[...]
    out_a = acc[:t, :V_HEAD_DIM] / acc[:t, V_HEAD_DIM:]
    out_b = acc[t:, :V_HEAD_DIM] / acc[t:, V_HEAD_DIM:]
    o = out_a - lam * out_b
    y = o * lax.rsqrt(jnp.mean(o * o, axis=-1, keepdims=True) + EPS)
    o_ref[...] = ((y * sub_ref[...]) * (1.0 - LAMBDA_INIT)).astype(o_ref.dtype)


def _diff_attention(h3, lam_vecs, bias_tiles, sub_norm):
    b, s, _ = h3.shape
    t = ATTN_Q_TILE
    tk = ATTN_K_TILE
    nh = N_ATTN_HEADS
    return pl.pallas_call(
        _attn_kernel,
        grid=(b, nh, s // t),
        in_specs=[
            pl.BlockSpec((4, QK_HEAD_DIM), lambda bi, hi, qi: (0, 0)),
            pl.BlockSpec((None, t, 128), lambda bi, hi, qi: (bi, qi, hi)),
            pl.BlockSpec((None, s, 128), lambda bi, hi, qi: (bi, 0, nh + hi)),
            pl.BlockSpec((None, s, 128), lambda bi, hi, qi: (bi, 0, 2 * nh + hi)),
            pl.BlockSpec((None, N_BIAS_TILES, t, tk), lambda bi, hi, qi: (hi, 0, 0, 0)),
            pl.BlockSpec((1, V_HEAD_DIM), lambda bi, hi, qi: (0, 0)),
        ],
        out_specs=pl.BlockSpec((None, t, 128), lambda bi, hi, qi: (bi, qi, hi)),
        out_shape=jax.ShapeDtypeStruct((b, s, D_ATTN), BF16),
        scratch_shapes=[pltpu.VMEM((2 * t, 2 * V_HEAD_DIM), F32),
                        pltpu.VMEM((2 * t, V_HEAD_DIM), F32),
                        pltpu.VMEM((2, 2 * t, tk), F32),
                        pltpu.VMEM((2, 2 * t, tk), BF16),
                        pltpu.VMEM((2, 2 * t, V_HEAD_DIM), F32),
                        pltpu.VMEM((2, 2 * t, V_HEAD_DIM), F32)],
        compiler_params=_params(("parallel", "parallel", "arbitrary")),
        name="diff_attention",
    )(lam_vecs, h3, h3, h3, bias_tiles, sub_norm)


def _attention_bias_tiles(rel_bias, s):
    t, tk = ATTN_Q_TILE, ATTN_K_TILE
    n = jnp.arange(FAR_DISTANCE, dtype=jnp.int32)
    max_exact = N_BUCKETS // 2
    nf = jnp.maximum(n, 1).astype(F32)
    large = max_exact + (jnp.log(nf / max_exact) / math.log(MAX_DISTANCE / max_exact)
                         * (N_BUCKETS - max_exact)).astype(jnp.int32)
    large = jnp.minimum(large, N_BUCKETS - 1)
    bucket = jnp.where(n < max_exact, n, large)
    far = np.arange(FAR_DISTANCE, s, dtype=np.float64)
    far_bucket = max_exact + np.floor(np.log(far / max_exact) / math.log(MAX_DISTANCE / max_exact)
                                      * (N_BUCKETS - max_exact) - 1e-6)
    assert far_bucket.min() >= N_BUCKETS - 1
    table = (rel_bias[bucket] - rel_bias[N_BUCKETS - 1][None, :]).T
    nh = table.shape[0]
    span = t + tk - 1
    lo = (tk - 1) + (t - tk)
    hi = (t - 1) + tk * (N_BIAS_TILES - 1) - (t - tk)
    by_dist = jnp.concatenate([jnp.full((nh, lo), MASK_VALUE, F32), table,
                               jnp.zeros((nh, hi + 1 - FAR_DISTANCE), F32)], axis=1)

    def toeplitz(g):
        c = jnp.concatenate([g[:, :tk][:, ::-1], g[:, tk:][:, ::-1]], axis=1)
        flat = jnp.tile(c, (1, t))[:, :t * (span - 1)]
        return flat.reshape(nh, t, span - 1)[:, :, :tk]

    tiles = []
    for idx in range(N_BIAS_TILES):
        start = tk * idx - (t - tk) - (tk - 1) + lo
        tiles.append(toeplitz(by_dist[:, start:start + span]))
    return jnp.stack(tiles, axis=1).astype(F32)


SCAN_ROWS = 8
SCAN_COLS = 512


def _s5_kernel(u_ref, bre_ref, bim_ref, cre_ref, cim_ref, coef_ref, d_ref, wglu_ref,
               o_ref, xr_ref, xi_ref, carry_ref):
    tt = u_ref.shape[0]

    @pl.when(pl.program_id(1) == 0)
    def _():
        carry_ref[...] = jnp.zeros_like(carry_ref)

    u = u_ref[...]
    for j in range(N_OCTETS):
        uj = u[:, j * 128:(j + 1) * 128]
        cols = slice(j * OCTET_STATES, (j + 1) * OCTET_STATES)
        xr_ref[:, cols] = jnp.dot(uj, bre_ref[j], preferred_element_type=F32)
        xi_ref[:, cols] = jnp.dot(uj, bim_ref[j], preferred_element_type=F32)

    for c in range(N_CPLX // SCAN_COLS):
        cols = slice(c * SCAN_COLS, (c + 1) * SCAN_COLS)
        coef = [coef_ref[i, :, cols] for i in range(8)]

        def body(r, carry, cols=cols, coef=coef):
            cr, ci = carry
            rows = pl.ds(pl.multiple_of(r * SCAN_ROWS, SCAN_ROWS), SCAN_ROWS)
            xr = xr_ref[rows, cols]
            xi = xi_ref[rows, cols]
            for idx, k in enumerate((1, 2, 4)):
                ar, ai = coef[2 * idx], coef[2 * idx + 1]
                sr = pltpu.roll(xr, k, 0)
                si = pltpu.roll(xi, k, 0)
                xr, xi = xr + ar * sr - ai * si, xi + ar * si + ai * sr
            pr, pi = coef[6], coef[7]
            xr, xi = xr + pr * cr - pi * ci, xi + pr * ci + pi * cr
            xr_ref[rows, cols] = xr
            xi_ref[rows, cols] = xi
            return xr[SCAN_ROWS - 1:SCAN_ROWS, :], xi[SCAN_ROWS - 1:SCAN_ROWS, :]

        cr, ci = lax.fori_loop(0, tt // SCAN_ROWS, body,
                               (carry_ref[0:1, cols], carry_ref[1:2, cols]))
        carry_ref[0:1, cols] = cr
        carry_ref[1:2, cols] = ci

    ys = []
    for j in range(N_OCTETS):
        cols = slice(j * OCTET_STATES, (j + 1) * OCTET_STATES)
        ys.append(jnp.dot(xr_ref[:, cols].astype(BF16), cre_ref[j], preferred_element_type=F32)
                  + jnp.dot(xi_ref[:, cols].astype(BF16), cim_ref[j], preferred_element_type=F32))
    y = jnp.concatenate(ys, axis=1) + d_ref[...] * u.astype(F32)
    y = jax.nn.gelu(y).astype(BF16)
    z = jnp.dot(y, wglu_ref[...], preferred_element_type=F32)
    o_ref[...] = (z[:, :D_SSM] * jax.nn.sigmoid(z[:, D_SSM:])).astype(o_ref.dtype)


def _s5_mixer(h3, bre, bim, cre, cim, coef, d_row, w_glu, tt=256):
    b, s, _ = h3.shape
    const3 = lambda bi, ti: (0, 0, 0)
    const2 = lambda bi, ti: (0, 0)
    return pl.pallas_call(
        _s5_kernel,
        grid=(b, s // tt),
        in_specs=[
            pl.BlockSpec((None, tt, D_SSM), lambda bi, ti: (bi, ti, 3)),
            pl.BlockSpec(bre.shape, const3),
            pl.BlockSpec(bim.shape, const3),
            pl.BlockSpec(cre.shape, const3),
            pl.BlockSpec(cim.shape, const3),
            pl.BlockSpec(coef.shape, const3),
            pl.BlockSpec(d_row.shape, const2),
            pl.BlockSpec(w_glu.shape, const2),
        ],
        out_specs=pl.BlockSpec((None, tt, D_SSM), lambda bi, ti: (bi, ti, 0)),
        out_shape=jax.ShapeDtypeStruct((b, s, D_SSM), BF16),
        scratch_shapes=[pltpu.VMEM((tt, N_CPLX), F32),
                        pltpu.VMEM((tt, N_CPLX), F32),
                        pltpu.VMEM((8, N_CPLX), F32)],
        compiler_params=_params(("parallel", "arbitrary")),
        name="s5_mixer",
    )(h3, bre, bim, cre, cim, coef, d_row, w_glu)


def _s5_parameters(a_re, a_im, log_dt, b_re, b_im, c_re, c_im):
    g, p, hc = N_SSM_GROUPS, SSM_STATE, SSM_GROUP
    dt = jnp.exp(log_dt.astype(F32))[:, None]
    ar = a_re.astype(F32)
    ai = a_im.astype(F32)
    mag = jnp.exp(ar * dt)
    lb_re = mag * jnp.cos(ai * dt)
    lb_im = mag * jnp.sin(ai * dt)
    den = ar * ar + ai * ai
    nr, ni = lb_re - 1.0, lb_im
    f_re = (nr * ar + ni * ai) / den
    f_im = (ni * ar - nr * ai) / den
    br = b_re.astype(F32)
    bi = b_im.astype(F32)
    bb_re = f_re[..., None] * br - f_im[..., None] * bi
    bb_im = f_re[..., None] * bi + f_im[..., None] * br

    eye = jnp.eye(GROUPS_PER_OCTET, dtype=F32)

    def b_layout(bb):
        bb4 = bb.reshape(N_OCTETS, GROUPS_PER_OCTET, p, hc)
        return jnp.einsum('jgph,kg->jkhgp', bb4, eye).reshape(
            N_OCTETS, GROUPS_PER_OCTET * hc, OCTET_STATES).astype(BF16)

    def c_layout(cc):
        c4 = cc.astype(F32).reshape(N_OCTETS, GROUPS_PER_OCTET, hc, p)
        return jnp.einsum('jghp,kg->jgpkh', c4, eye).reshape(
            N_OCTETS, OCTET_STATES, GROUPS_PER_OCTET * hc).astype(BF16)

    def cmul(xr, xi, yr, yi):
        return xr * yr - xi * yi, xr * yi + xi * yr

    l1 = (lb_re.reshape(-1), lb_im.reshape(-1))
    pows = [l1]
    for _ in range(SCAN_ROWS - 1):
        pows.append(cmul(*pows[-1], *l1))
    row = jnp.arange(SCAN_ROWS)[:, None]
    coefs = []
    for k in (1, 2, 4):
        keep = (row >= k).astype(F32)
        coefs += [keep * pows[k - 1][0][None, :], keep * pows[k - 1][1][None, :]]
    coefs += [jnp.stack([pw[0] for pw in pows]), jnp.stack([pw[1] for pw in pows])]
    coef = jnp.stack(coefs)
    return b_layout(bb_re), b_layout(bb_im), c_layout(c_re), -c_layout(c_im), coef


def _out_proj_kernel(a_ref, s_ref, w_ref, x_ref, gpost_ref, gpre_ref, x1_ref, xn_ref):
    mix = (jnp.dot(a_ref[...], w_ref[:D_ATTN, :], preferred_element_type=F32)
           + jnp.dot(s_ref[...], w_ref[D_ATTN:, :], preferred_element_type=F32))
    y = mix * lax.rsqrt(jnp.mean(mix * mix, axis=-1, keepdims=True) + EPS)
    x1 = x_ref[...] + y * gpost_ref[...]
    x1_ref[...] = x1
    z = x1 * lax.rsqrt(jnp.mean(x1 * x1, axis=-1, keepdims=True) + EPS)
    xn_ref[...] = (z * gpre_ref[...]).astype(xn_ref.dtype)


def _out_proj(attn2, ssm2, w_out, x2, g_post, g_pre, tm=256):
    m, d = x2.shape
    return pl.pallas_call(
        _out_proj_kernel,
        grid=(m // tm,),
        in_specs=[
            pl.BlockSpec((tm, D_ATTN), lambda i: (i, 0)),
            pl.BlockSpec((tm, D_SSM), lambda i: (i, 0)),
            pl.BlockSpec(w_out.shape, lambda i: (0, 0)),
            pl.BlockSpec((tm, d), lambda i: (i, 0)),
            pl.BlockSpec((1, d), lambda i: (0, 0)),
            pl.BlockSpec((1, d), lambda i: (0, 0)),
        ],
        out_specs=[pl.BlockSpec((tm, d), lambda i: (i, 0)),
                   pl.BlockSpec((tm, d), lambda i: (i, 0))],
        out_shape=[jax.ShapeDtypeStruct((m, d), F32),
                   jax.ShapeDtypeStruct((m, d), BF16)],
        compiler_params=_params(("parallel",)),
        name="out_proj_norm",
    )(attn2, ssm2, w_out, x2, g_post, g_pre)


CONV_HALO = 8


def _up_conv_kernel(x_ref, wg_ref, wv_ref, cwg_ref, cwv_ref, cbg_ref, cbv_ref, o_ref,
                    hg_ref, hv_ref, *, tiles_per_seq):
    tm = x_ref.shape[0]
    seq_start = (pl.program_id(1) % tiles_per_seq) == 0
    x = x_ref[...]

    def conv(h_ref, w_ref, cw_ref, cb_ref):
        h = jnp.dot(x, w_ref[...], preferred_element_type=F32)
        @pl.when(seq_start)
        def _():
            h_ref[0:CONV_HALO, :] = jnp.zeros((CONV_HALO, h_ref.shape[1]), F32)

        @pl.when(jnp.logical_not(seq_start))
        def _():
            h_ref[0:CONV_HALO, :] = h_ref[tm:tm + CONV_HALO, :]

        h_ref[CONV_HALO:, :] = h
        hm1 = h_ref[CONV_HALO - 1:CONV_HALO - 1 + tm, :]
        hm2 = h_ref[CONV_HALO - 2:CONV_HALO - 2 + tm, :]
        return cw_ref[0:1, :] * hm2 + cw_ref[1:2, :] * hm1 + cw_ref[2:3, :] * h + cb_ref[...]

    g = conv(hg_ref, wg_ref, cwg_ref, cbg_ref)
    v = conv(hv_ref, wv_ref, cwv_ref, cbv_ref)
    o_ref[...] = (jax.nn.silu(g) * v).astype(o_ref.dtype)


def _up_conv(xn2, w_up, conv_w, conv_b, seq_len, tm=512, tn=512):
    m, d = xn2.shape
    nj = D_FF // tn
    kern = functools.partial(_up_conv_kernel, tiles_per_seq=seq_len // tm)
    return pl.pallas_call(
        kern,
        grid=(nj, m // tm),
        in_specs=[
            pl.BlockSpec((tm, d), lambda j, i: (i, 0)),
            pl.BlockSpec((d, tn), lambda j, i: (0, j)),
            pl.BlockSpec((d, tn), lambda j, i: (0, nj + j)),
            pl.BlockSpec((3, tn), lambda j, i: (0, j)),
            pl.BlockSpec((3, tn), lambda j, i: (0, nj + j)),
            pl.BlockSpec((1, tn), lambda j, i: (0, j)),
            pl.BlockSpec((1, tn), lambda j, i: (0, nj + j)),
        ],
        out_specs=pl.BlockSpec((tm, tn), lambda j, i: (i, j)),
        out_shape=jax.ShapeDtypeStruct((m, D_FF), BF16),
        scratch_shapes=[pltpu.VMEM((tm + CONV_HALO, tn), F32),
                        pltpu.VMEM((tm + CONV_HALO, tn), F32)],
        compiler_params=_params(("parallel", "arbitrary")),
        name="up_conv_gate",
    )(xn2, w_up, w_up, conv_w, conv_w, conv_b, conv_b)


def _down_proj_kernel(a_ref, w_ref, x_ref, g_ref, o_ref, acc_ref):
    k = pl.program_id(1)

    @pl.when(k == 0)
    def _():
        acc_ref[...] = jnp.zeros_like(acc_ref)

    acc_ref[...] += jnp.dot(a_ref[...], w_ref[...], preferred_element_type=F32)

    @pl.when(k == pl.num_programs(1) - 1)
    def _():
        f = acc_ref[...]
        y = f * lax.rsqrt(jnp.mean(f * f, axis=-1, keepdims=True) + EPS)
        o_ref[...] = x_ref[...] + y * g_ref[...]


def _down_proj(gated, w_down, x1, g_post, tm=512, tk=512):
    m, kdim = gated.shape
    d = w_down.shape[1]
    return pl.pallas_call(
        _down_proj_kernel,
        grid=(m // tm, kdim // tk),
        in_specs=[
            pl.BlockSpec((tm, tk), lambda i, k: (i, k)),
            pl.BlockSpec((tk, d), lambda i, k: (k, 0)),
            pl.BlockSpec((tm, d), lambda i, k: (i, 0)),
            pl.BlockSpec((1, d), lambda i, k: (0, 0)),
        ],
        out_specs=pl.BlockSpec((tm, d), lambda i, k: (i, 0)),
        out_shape=jax.ShapeDtypeStruct((m, d), F32),
        scratch_shapes=[pltpu.VMEM((tm, d), F32)],
        compiler_params=_params(("parallel", "arbitrary")),
        name="down_proj_norm",
    )(gated, w_down, x1, g_post)


def kernel(x, ln_pre_mix, ln_post_mix, ln_pre_ffn, ln_post_ffn, w_in, lam_q1, lam_k1, lam_q2, lam_k2, attn_sub_norm, rel_bias, ssm_a_re, ssm_a_im, ssm_log_dt, ssm_b_re, ssm_b_im, ssm_c_re, ssm_c_im, ssm_d, ssm_w_glu, w_out, w_up, conv_w, conv_b, w_down):
    b, s, d = x.shape
    assert d == D_MODEL and s % ATTN_Q_TILE == 0 and ATTN_Q_TILE % ATTN_K_TILE == 0
    l = 0
    x2 = x.reshape(b * s, d)

    h = _norm_proj(x2, ln_pre_mix[l][None, :], w_in[l].astype(BF16))
    h3 = h.reshape(b, s, D_IN)

    lam_vecs = jnp.stack([lam_q1[l], lam_k1[l], lam_q2[l], lam_k2[l]]).astype(F32)
    bias_tiles = _attention_bias_tiles(rel_bias.astype(F32), s)
    attn = _diff_attention(h3, lam_vecs, bias_tiles, attn_sub_norm[l][None, :].astype(F32))

    bre, bim, cre, cim, coef = _s5_parameters(ssm_a_re[l], ssm_a_im[l], ssm_log_dt[l],
                                              ssm_b_re[l], ssm_b_im[l], ssm_c_re[l], ssm_c_im[l])
    ssm = _s5_mixer(h3, bre, bim, cre, cim, coef,
                    ssm_d[l].reshape(1, D_SSM).astype(F32), ssm_w_glu[l].astype(BF16))

    x1, xn2 = _out_proj(attn.reshape(b * s, D_ATTN), ssm.reshape(b * s, D_SSM),
                        w_out[l].astype(BF16), x2, ln_post_mix[l][None, :], ln_pre_ffn[l][None, :])

    gated = _up_conv(xn2, w_up[l].astype(BF16), conv_w[l].reshape(3, 2 * D_FF),
                     conv_b[l].reshape(1, 2 * D_FF), s)
    y = _down_proj(gated, w_down[l].astype(BF16), x1, ln_post_ffn[l][None, :])
    return y.reshape(b, s, d)
```

```python
import functools
import math

import numpy as np
import jax
import jax.numpy as jnp
from jax import lax
from jax.experimental import pallas as pl
from jax.experimental.pallas import tpu as pltpu

D_MODEL = 2048
D_ATTN = 1024
D_SSM = 1024
QK_HEAD_DIM = 64
V_HEAD_DIM = 128
N_ATTN_HEADS = 8
D_QK = 1024
SSM_GROUP = 16
N_SSM_GROUPS = 64
SSM_STATE = 64
D_IN = 4096
D_FF = 5632
N_BUCKETS = 32
MAX_DISTANCE = 128
EPS = 1e-6
LAMBDA_INIT = 0.8 - 0.6 * math.exp(-0.3 * 0)

N_CPLX = N_SSM_GROUPS * SSM_STATE
GROUPS_PER_OCTET = 8
N_OCTETS = N_SSM_GROUPS // GROUPS_PER_OCTET
OCTET_STATES = GROUPS_PER_OCTET * SSM_STATE

ATTN_Q_TILE = 512
ATTN_K_TILE = 256
FAR_DISTANCE = 2 * MAX_DISTANCE + 1
N_BIAS_TILES = -(-(FAR_DISTANCE + ATTN_Q_TILE - 1) // ATTN_K_TILE)
MASK_VALUE = -1e30

VMEM_LIMIT = 56 * 1024 * 1024

F32 = jnp.float32
BF16 = jnp.bfloat16


def _params(sem, flags=None):
    return pltpu.CompilerParams(dimension_semantics=sem, vmem_limit_bytes=VMEM_LIMIT, flags=flags)


def _norm_proj_kernel(x_ref, g_ref, w_ref, o_ref, xn_ref):
    @pl.when(pl.program_id(1) == 0)
    def _():
        x = x_ref[...]
        y = x * lax.rsqrt(jnp.mean(x * x, axis=-1, keepdims=True) + EPS)
        xn_ref[...] = (y * g_ref[...]).astype(BF16)

    o_ref[...] = jnp.dot(xn_ref[...], w_ref[...], preferred_element_type=F32).astype(o_ref.dtype)


def _norm_proj(x2, g, w, tm=512, tn=1024):
    m, d = x2.shape
    n = w.shape[1]
    return pl.pallas_call(
        _norm_proj_kernel,
        grid=(m // tm, n // tn),
        in_specs=[
            pl.BlockSpec((tm, d), lambda i, j: (i, 0)),
            pl.BlockSpec((1, d), lambda i, j: (0, 0)),
            pl.BlockSpec((d, tn), lambda i, j: (0, j)),
        ],
        out_specs=pl.BlockSpec((tm, tn), lambda i, j: (i, j)),
        out_shape=jax.ShapeDtypeStruct((m, n), BF16),
        scratch_shapes=[pltpu.VMEM((tm, d), BF16)],
        compiler_params=_params(("parallel", "arbitrary")),
        name="norm_in_proj",
    )(x2, g, w)


def _attn_kernel(lam_ref, q_ref, k_ref, v_ref, bias_ref, sub_ref, o_ref,
                 acc_ref, m_ref, s_ref, p_ref, alpha_ref, mt_ref):
    t = ATTN_Q_TILE
    tk = ATTN_K_TILE
    qi = pl.program_id(2)
    last = (t // tk) * (qi + 1) - 1

    q = q_ref[...]
    lane = lax.broadcasted_iota(jnp.int32, q.shape, 1)
    zero = jnp.zeros_like(q)
    scale = jnp.asarray(QK_HEAD_DIM ** -0.5, q.dtype)
    q2 = jnp.concatenate([jnp.where(lane < QK_HEAD_DIM, q, zero),
                          jnp.where(lane >= QK_HEAD_DIM, q, zero)], axis=0) * scale

    ones = jnp.ones((tk, V_HEAD_DIM), BF16)

    def scores(kt):
        k = k_ref[pl.ds(pl.multiple_of(kt * tk, tk), tk), :]
        return lax.dot_general(q2, k, (((1,), (1,)), ((), ())), preferred_element_type=F32)

    def v_ext(kt):
        v = v_ref[pl.ds(pl.multiple_of(kt * tk, tk), tk), :]
        return jnp.concatenate([v, ones], axis=1)

    def row_max(s):
        m = jnp.max(jnp.maximum(s[:, :V_HEAD_DIM], s[:, V_HEAD_DIM:]), axis=-1, keepdims=True)
        return jnp.broadcast_to(m, (2 * t, V_HEAD_DIM))

    def store_scores(kt, slot, bias_idx=None):
        s = scores(kt)
        if bias_idx is not None:
            bias = bias_ref[bias_idx]
            s = s + jnp.concatenate([bias, bias], axis=0)
        s_ref[slot] = s
        mt_ref[slot] = row_max(s)

    def softmax_stage(slot):
        s = s_ref[slot]
        m_old = m_ref[...]
        m_new = jnp.maximum(m_old, mt_ref[slot])
        alpha_ref[slot] = jnp.exp(m_old - m_new)
        m_ref[...] = m_new
        p_ref[slot] = jnp.exp(s - jnp.concatenate([m_new, m_new], axis=1)).astype(BF16)

    def pv_stage(kt, slot):
        a = alpha_ref[slot]
        acc_ref[...] = (acc_ref[...] * jnp.concatenate([a, a], axis=1)
                        + jnp.dot(p_ref[slot], v_ext(kt), preferred_element_type=F32))

    def step(j, slot, bias_idx=None):
        pv_stage(jnp.maximum(j - 1, 0), 1 - slot)
        softmax_stage(slot)
        store_scores(j + 1, 1 - slot, bias_idx)

    m_ref[...] = jnp.full(m_ref.shape, MASK_VALUE, F32)
    acc_ref[...] = jnp.zeros(acc_ref.shape, F32)
    p_ref[1] = jnp.zeros(p_ref.shape[1:], BF16)
    alpha_ref[1] = jnp.ones(alpha_ref.shape[1:], F32)

    assert N_BIAS_TILES == 3 and t == 2 * tk

    @pl.when(qi == 0)
    def _():
        store_scores(0, 0, bias_idx=1)

    @pl.when(qi > 0)
    def _():
        store_scores(0, 0)

    def pair_body(i, carry):
        step(2 * i, 0)
        step(2 * i + 1, 1)
        return carry

    lax.fori_loop(0, jnp.maximum(qi - 1, 0), pair_body, 0)

    @pl.when(qi > 0)
    def _():
        step(last - 3, 0, bias_idx=2)
        step(last - 2, 1, bias_idx=1)

    step(last - 1, 0, bias_idx=0)
    pv_stage(last - 1, 0)
    softmax_stage(1)
    pv_stage(last, 1)

    lam_v = lam_ref[...]
    lam = (jnp.exp(jnp.sum(lam_v[0:1] * lam_v[1:2], axis=-1, keepdims=True))
           - jnp.exp(jnp.sum(lam_v[2:3] * lam_v[3:4], axis=-1, keepdims=True))
           + LAMBDA_INIT)
    acc = acc_ref[...]
    out_a = acc[:t, :V_HEAD_DIM] / acc[:t, V_HEAD_DIM:]
    out_b = acc[t:, :V_HEAD_DIM] / acc[t:, V_HEAD_DIM:]
    o = out_a - lam * out_b
    y = o * lax.rsqrt(jnp.mean(o * o, axis=-1, keepdims=True) + EPS)
    o_ref[...] = ((y * sub_ref[...]) * (1.0 - LAMBDA_INIT)).astype(o_ref.dtype)


def _diff_attention(h3, lam_vecs, bias_tiles, sub_norm):
    b, s, _ = h3.shape
    t = ATTN_Q_TILE
    tk = ATTN_K_TILE
    nh = N_ATTN_HEADS
    return pl.pallas_call(
        _attn_kernel,
        grid=(b, nh, s // t),
        in_specs=[
            pl.BlockSpec((4, QK_HEAD_DIM), lambda bi, hi, qi: (0, 0)),
            pl.BlockSpec((None, t, 128), lambda bi, hi, qi: (bi, qi, hi)),
            pl.BlockSpec((None, s, 128), lambda bi, hi, qi: (bi, 0, nh + hi)),
            pl.BlockSpec((None, s, 128), lambda bi, hi, qi: (bi, 0, 2 * nh + hi)),
            pl.BlockSpec((None, N_BIAS_TILES, t, tk), lambda bi, hi, qi: (hi, 0, 0, 0)),
            pl.BlockSpec((1, V_HEAD_DIM), lambda bi, hi, qi: (0, 0)),
        ],
        out_specs=pl.BlockSpec((None, t, 128), lambda bi, hi, qi: (bi, qi, hi)),
        out_shape=jax.ShapeDtypeStruct((b, s, D_ATTN), BF16),
        scratch_shapes=[pltpu.VMEM((2 * t, 2 * V_HEAD_DIM), F32),
                        pltpu.VMEM((2 * t, V_HEAD_DIM), F32),
                        pltpu.VMEM((2, 2 * t, tk), F32),
                        pltpu.VMEM((2, 2 * t, tk), BF16),
                        pltpu.VMEM((2, 2 * t, V_HEAD_DIM), F32),
                        pltpu.VMEM((2, 2 * t, V_HEAD_DIM), F32)],
        compiler_params=_params(("parallel", "parallel", "arbitrary")),
        name="diff_attention",
    )(lam_vecs, h3, h3, h3, bias_tiles, sub_norm)


def _attention_bias_tiles(rel_bias, s):
    t, tk = ATTN_Q_TILE, ATTN_K_TILE
    nh = rel_bias.shape[1]
    assert _BUCKET_START[-1] < FAR_DISTANCE <= s
    return pl.pallas_call(
        _bias_tiles_kernel,
        grid=(nh,),
        in_specs=[pl.BlockSpec(memory_space=pltpu.SMEM)],
        out_specs=pl.BlockSpec((None, N_BIAS_TILES, t, tk), lambda h: (h, 0, 0, 0)),
        out_shape=jax.ShapeDtypeStruct((nh, N_BIAS_TILES, t, tk), F32),
        compiler_params=_params(("parallel",)),
        name="attention_bias_tiles",
    )(rel_bias)


def _bucket_starts():
    max_exact = N_BUCKETS // 2
    n = np.arange(max_exact, 4 * MAX_DISTANCE, dtype=np.float64)
    x = np.log(n / max_exact) / math.log(MAX_DISTANCE / max_exact) * (N_BUCKETS - max_exact)
    frac = np.abs(x - np.round(x))[1:]
    assert frac[n[1:] < MAX_DISTANCE].min() > 1e-3
    bucket = np.minimum(max_exact + np.floor(x + 1e-9).astype(np.int64), N_BUCKETS - 1)
    starts = list(range(max_exact)) + [int(n[np.argmax(bucket >= b)]) for b in range(max_exact, N_BUCKETS)]
    assert all(b > a for a, b in zip(starts, starts[1:]))
    return tuple(starts)


_BUCKET_START = _bucket_starts()
BIAS_ROWS = 64


def _bias_tiles_kernel(rel_ref, o_ref):
    t, tk = ATTN_Q_TILE, ATTN_K_TILE
    h = pl.program_id(0)
    far_value = rel_ref[N_BUCKETS - 1, h]
    values = [rel_ref[b, h] - far_value for b in range(N_BUCKETS - 1)]
    row = lax.broadcasted_iota(jnp.int32, (BIAS_ROWS, tk), 0)
    col = lax.broadcasted_iota(jnp.int32, (BIAS_ROWS, tk), 1)
    for idx in range(N_BIAS_TILES):
        for r in range(0, t, BIAS_ROWS):
            dist = row - col + (r + tk * idx - (t - tk))
            bias = jnp.zeros((BIAS_ROWS, tk), F32)
            for b in range(N_BUCKETS - 2, -1, -1):
                bias = jnp.where(dist < _BUCKET_START[b + 1], values[b], bias)
            o_ref[idx, r:r + BIAS_ROWS, :] = jnp.where(dist < 0, MASK_VALUE, bias)


N_STATE_TILES = N_CPLX // 128
TILES_PER_OCTET = OCTET_STATES // 128
SCAN_PITCH = 40
SUBLANES = 8


def _s5_kernel(u_ref, bre_ref, bim_ref, cre_ref, cim_ref, lam_ref, d_ref, wglu_ref,
               o_ref, xr_ref, xi_ref, carry_ref):
    tt = u_ref.shape[0]

    @pl.when(pl.program_id(1) == 0)
    def _():
        carry_ref[...] = jnp.zeros_like(carry_ref)

    def time_rows(t0, tile):
        return pl.ds(t0 * SCAN_PITCH + tile, SUBLANES, stride=SCAN_PITCH)

    u = u_ref[...]
    for j in range(N_OCTETS):
        uj = u[:, j * 128:(j + 1) * 128]
        for x_ref, b_ref in ((xr_ref, bre_ref), (xi_ref, bim_ref)):
            bu = jnp.dot(uj, b_ref[j], preferred_element_type=F32)
            for t0 in range(0, tt, SUBLANES):
                for c in range(TILES_PER_OCTET):
                    x_ref[time_rows(t0, TILES_PER_OCTET * j + c), :] = (
                        bu[t0:t0 + SUBLANES, c * 128:(c + 1) * 128])

    lam_r = lam_ref[0]
    lam_i = lam_ref[1]

    def body(t, carry):
        sr, si = carry
        rows = pl.ds(pl.multiple_of(t * SCAN_PITCH, SUBLANES), N_STATE_TILES)
        nr = lam_r * sr - lam_i * si + xr_ref[rows, :]
        ni = lam_r * si + lam_i * sr + xi_ref[rows, :]
        xr_ref[rows, :] = nr
        xi_ref[rows, :] = ni
        return nr, ni

    sr, si = lax.fori_loop(0, tt, body, (carry_ref[0], carry_ref[1]), unroll=2)
    carry_ref[0] = sr
    carry_ref[1] = si

    def natural(x_ref, j):
        return jnp.concatenate(
            [jnp.concatenate([x_ref[time_rows(t0, TILES_PER_OCTET * j + c), :]
                              for c in range(TILES_PER_OCTET)], axis=1)
             for t0 in range(0, tt, SUBLANES)], axis=0)

    ys = []
    for j in range(N_OCTETS):
        ys.append(jnp.dot(natural(xr_ref, j).astype(BF16), cre_ref[j], preferred_element_type=F32)
                  + jnp.dot(natural(xi_ref, j).astype(BF16), cim_ref[j], preferred_element_type=F32))
    y = jnp.concatenate(ys, axis=1) + d_ref[...] * u.astype(F32)
    y = jax.nn.gelu(y).astype(BF16)
    z = jnp.dot(y, wglu_ref[...], preferred_element_type=F32)
    o_ref[...] = (z[:, :D_SSM] * jax.nn.sigmoid(z[:, D_SSM:])).astype(o_ref.dtype)


def _s5_mixer(h3, bre, bim, cre, cim, lam, d_row, w_glu, tt=256):
    b, s, _ = h3.shape
    const3 = lambda bi, ti: (0, 0, 0)
    const2 = lambda bi, ti: (0, 0)
    return pl.pallas_call(
        _s5_kernel,
        grid=(b, s // tt),
        in_specs=[
            pl.BlockSpec((None, tt, D_SSM), lambda bi, ti: (bi, ti, 3)),
            pl.BlockSpec(bre.shape, const3),
            pl.BlockSpec(bim.shape, const3),
            pl.BlockSpec(cre.shape, const3),
            pl.BlockSpec(cim.shape, const3),
            pl.BlockSpec(lam.shape, const3),
            pl.BlockSpec(d_row.shape, const2),
            pl.BlockSpec(w_glu.shape, const2),
        ],
        out_specs=pl.BlockSpec((None, tt, D_SSM), lambda bi, ti: (bi, ti, 0)),
        out_shape=jax.ShapeDtypeStruct((b, s, D_SSM), BF16),
        scratch_shapes=[pltpu.VMEM((tt * SCAN_PITCH, 128), F32),
                        pltpu.VMEM((tt * SCAN_PITCH, 128), F32),
                        pltpu.VMEM((2, N_STATE_TILES, 128), F32)],
        compiler_params=_params(("parallel", "arbitrary")),
        name="s5_mixer",
    )(h3, bre, bim, cre, cim, lam, d_row, w_glu)


def _s5_parameters(a_re, a_im, log_dt, b_re, b_im, c_re, c_im):
    g, p, hc = N_SSM_GROUPS, SSM_STATE, SSM_GROUP
    dt = jnp.exp(log_dt.astype(F32))[:, None]
    ar = a_re.astype(F32)
    ai = a_im.astype(F32)
    mag = jnp.exp(ar * dt)
    lb_re = mag * jnp.cos(ai * dt)
    lb_im = mag * jnp.sin(ai * dt)
    den = ar * ar + ai * ai
    nr, ni = lb_re - 1.0, lb_im
    f_re = (nr * ar + ni * ai) / den
    f_im = (ni * ar - nr * ai) / den
    br = b_re.astype(F32)
    bi = b_im.astype(F32)
    bb_re = f_re[..., None] * br - f_im[..., None] * bi
    bb_im = f_re[..., None] * bi + f_im[..., None] * br

    eye = jnp.eye(GROUPS_PER_OCTET, dtype=F32)

    def b_layout(bb):
        bb4 = bb.reshape(N_OCTETS, GROUPS_PER_OCTET, p, hc)
        return jnp.einsum('jgph,kg->jkhgp', bb4, eye).reshape(
            N_OCTETS, GROUPS_PER_OCTET * hc, OCTET_STATES).astype(BF16)

    def c_layout(cc):
        c4 = cc.astype(F32).reshape(N_OCTETS, GROUPS_PER_OCTET, hc, p)
        return jnp.einsum('jghp,kg->jgpkh', c4, eye).reshape(
            N_OCTETS, OCTET_STATES, GROUPS_PER_OCTET * hc).astype(BF16)

    lam = jnp.stack([lb_re, lb_im]).reshape(2, N_STATE_TILES, 128)
    return b_layout(bb_re), b_layout(bb_im), c_layout(c_re), -c_layout(c_im), lam


def _out_proj_kernel(a_ref, s_ref, w_ref, x_ref, gpost_ref, gpre_ref, x1_ref, xn_ref):
    mix = (jnp.dot(a_ref[...], w_ref[:D_ATTN, :], preferred_element_type=F32)
           + jnp.dot(s_ref[...], w_ref[D_ATTN:, :], preferred_element_type=F32))
    y = mix * lax.rsqrt(jnp.mean(mix * mix, axis=-1, keepdims=True) + EPS)
    x1 = x_ref[...] + y * gpost_ref[...]
    x1_ref[...] = x1
    z = x1 * lax.rsqrt(jnp.mean(x1 * x1, axis=-1, keepdims=True) + EPS)
    xn_ref[...] = (z * gpre_ref[...]).astype(xn_ref.dtype)


def _out_proj(attn2, ssm2, w_out, x2, g_post, g_pre, tm=256):
    m, d = x2.shape
    return pl.pallas_call(
        _out_proj_kernel,
        grid=(m // tm,),
        in_specs=[
            pl.BlockSpec((tm, D_ATTN), lambda i: (i, 0)),
            pl.BlockSpec((tm, D_SSM), lambda i: (i, 0)),
            pl.BlockSpec(w_out.shape, lambda i: (0, 0)),
            pl.BlockSpec((tm, d), lambda i: (i, 0)),
            pl.BlockSpec((1, d), lambda i: (0, 0)),
            pl.BlockSpec((1, d), lambda i: (0, 0)),
        ],
        out_specs=[pl.BlockSpec((tm, d), lambda i: (i, 0)),
                   pl.BlockSpec((tm, d), lambda i: (i, 0))],
        out_shape=[jax.ShapeDtypeStruct((m, d), F32),
                   jax.ShapeDtypeStruct((m, d), BF16)],
        compiler_params=_params(("parallel",)),
        name="out_proj_norm",
    )(attn2, ssm2, w_out, x2, g_post, g_pre)


CONV_HALO = 8
CONV_CHUNK = 256
CONV_ROWS = 64


def _up_conv_kernel(x_ref, wg_ref, wv_ref, cwg_ref, cwv_ref, cbg_ref, cbv_ref, o_ref,
                    h_ref, *, tiles_per_seq):
    tm = x_ref.shape[0]
    tn = o_ref.shape[1]
    n_chunks = tn // CONV_CHUNK
    i = pl.program_id(1)
    cur = i % 2
    prev = 1 - cur

    @pl.when(i == 0)
    def _():
        h_ref[...] = jnp.zeros(h_ref.shape, F32)

    seq_start = (i % tiles_per_seq) == 0
    for k in range(2 * n_chunks):
        tail = h_ref[k, prev, tm:tm + CONV_HALO, :]
        h_ref[k, cur, 0:CONV_HALO, :] = jnp.where(seq_start, 0.0, tail)

    x = x_ref[...]

    def matmul(c):
        cols = slice(c * CONV_CHUNK, (c + 1) * CONV_CHUNK)
        for k, w_ref in ((2 * c, wg_ref), (2 * c + 1, wv_ref)):
            h_ref[k, cur, CONV_HALO:, :] = jnp.dot(x, w_ref[:, cols], preferred_element_type=F32)

    def conv_gate(c):
        cols = slice(c * CONV_CHUNK, (c + 1) * CONV_CHUNK)

        def conv(k, cw_ref, cb_ref, r):
            rows = lambda shift: slice(CONV_HALO - shift + r, CONV_HALO - shift + r + CONV_ROWS)
            return (cw_ref[0:1, cols] * h_ref[k, prev, rows(2), :]
                    + cw_ref[1:2, cols] * h_ref[k, prev, rows(1), :]
                    + cw_ref[2:3, cols] * h_ref[k, prev, rows(0), :] + cb_ref[:, cols])

        for r in range(0, tm, CONV_ROWS):
            g = conv(2 * c, cwg_ref, cbg_ref, r)
            v = conv(2 * c + 1, cwv_ref, cbv_ref, r)
            o_ref[r:r + CONV_ROWS, cols] = (jax.nn.silu(g) * v).astype(o_ref.dtype)

    for c in range(n_chunks):
        conv_gate(c)
        matmul(c)


def _up_conv(xn2, w_up, conv_w, conv_b, seq_len, tm=512, tn=512):
    m, d = xn2.shape
    nj = D_FF // tn
    ni = m // tm
    kern = functools.partial(_up_conv_kernel, tiles_per_seq=seq_len // tm)
    return pl.pallas_call(
        kern,
        grid=(nj, ni + 1),
        in_specs=[
            pl.BlockSpec((tm, d), lambda j, i: (jnp.minimum(i, ni - 1), 0)),
            pl.BlockSpec((d, tn), lambda j, i: (0, j)),
            pl.BlockSpec((d, tn), lambda j, i: (0, nj + j)),
            pl.BlockSpec((3, tn), lambda j, i: (0, j)),
            pl.BlockSpec((3, tn), lambda j, i: (0, nj + j)),
            pl.BlockSpec((1, tn), lambda j, i: (0, j)),
            pl.BlockSpec((1, tn), lambda j, i: (0, nj + j)),
        ],
        out_specs=pl.BlockSpec((tm, tn), lambda j, i: (jnp.maximum(i - 1, 0), j)),
        out_shape=jax.ShapeDtypeStruct((m, D_FF), BF16),
        scratch_shapes=[pltpu.VMEM((2 * (tn // CONV_CHUNK), 2, tm + CONV_HALO, CONV_CHUNK), F32)],
        compiler_params=_params(("parallel", "arbitrary")),
        name="up_conv_gate",
    )(xn2, w_up, w_up, conv_w, conv_w, conv_b, conv_b)


def _down_proj_kernel(a_ref, w_ref, x_ref, g_ref, o_ref, acc_ref):
    k = pl.program_id(1)

    @pl.when(k == 0)
    def _():
        acc_ref[...] = jnp.zeros_like(acc_ref)

    acc_ref[...] += jnp.dot(a_ref[...], w_ref[...], preferred_element_type=F32)

    @pl.when(k == pl.num_programs(1) - 1)
    def _():
        f = acc_ref[...]
        y = f * lax.rsqrt(jnp.mean(f * f, axis=-1, keepdims=True) + EPS)
        o_ref[...] = x_ref[...] + y * g_ref[...]


def _down_proj(gated, w_down, x1, g_post, tm=512, tk=512):
    m, kdim = gated.shape
    d = w_down.shape[1]
    return pl.pallas_call(
        _down_proj_kernel,
        grid=(m // tm, kdim // tk),
        in_specs=[
            pl.BlockSpec((tm, tk), lambda i, k: (i, k)),
            pl.BlockSpec((tk, d), lambda i, k: (k, 0)),
            pl.BlockSpec((tm, d), lambda i, k: (i, 0)),
            pl.BlockSpec((1, d), lambda i, k: (0, 0)),
        ],
        out_specs=pl.BlockSpec((tm, d), lambda i, k: (i, 0)),
        out_shape=jax.ShapeDtypeStruct((m, d), F32),
        scratch_shapes=[pltpu.VMEM((tm, d), F32)],
        compiler_params=_params(("parallel", "arbitrary")),
        name="down_proj_norm",
    )(gated, w_down, x1, g_post)


def kernel(x, ln_pre_mix, ln_post_mix, ln_pre_ffn, ln_post_ffn, w_in, lam_q1, lam_k1, lam_q2, lam_k2, attn_sub_norm, rel_bias, ssm_a_re, ssm_a_im, ssm_log_dt, ssm_b_re, ssm_b_im, ssm_c_re, ssm_c_im, ssm_d, ssm_w_glu, w_out, w_up, conv_w, conv_b, w_down):
    b, s, d = x.shape
    assert d == D_MODEL and s % ATTN_Q_TILE == 0 and ATTN_Q_TILE % ATTN_K_TILE == 0
    l = 0
    x2 = x.reshape(b * s, d)

    h = _norm_proj(x2, ln_pre_mix[l][None, :], w_in[l].astype(BF16))
    h3 = h.reshape(b, s, D_IN)

    lam_vecs = jnp.stack([lam_q1[l], lam_k1[l], lam_q2[l], lam_k2[l]]).astype(F32)
    bias_tiles = _attention_bias_tiles(rel_bias.astype(F32), s)
    attn = _diff_attention(h3, lam_vecs, bias_tiles, attn_sub_norm[l][None, :].astype(F32))

    bre, bim, cre, cim, coef = _s5_parameters(ssm_a_re[l], ssm_a_im[l], ssm_log_dt[l],
                                              ssm_b_re[l], ssm_b_im[l], ssm_c_re[l], ssm_c_im[l])
    ssm = _s5_mixer(h3, bre, bim, cre, cim, coef,
                    ssm_d[l].reshape(1, D_SSM).astype(F32), ssm_w_glu[l].astype(BF16))

    x1, xn2 = _out_proj(attn.reshape(b * s, D_ATTN), ssm.reshape(b * s, D_SSM),
                        w_out[l].astype(BF16), x2, ln_post_mix[l][None, :], ln_pre_ffn[l][None, :])

    gated = _up_conv(xn2, w_up[l].astype(BF16), conv_w[l].reshape(3, 2 * D_FF),
                     conv_b[l].reshape(1, 2 * D_FF), s)
    y = _down_proj(gated, w_down[l].astype(BF16), x1, ln_post_ffn[l][None, :])
    return y.reshape(b, s, d)
```

```python
import functools
import math

import numpy as np
import jax
import jax.numpy as jnp
from jax import lax
from jax.experimental import pallas as pl
from jax.experimental.pallas import tpu as pltpu

D_MODEL = 2048
D_ATTN = 1024
D_SSM = 1024
QK_HEAD_DIM = 64
V_HEAD_DIM = 128
N_ATTN_HEADS = 8
D_QK = 1024
SSM_GROUP = 16
N_SSM_GROUPS = 64
SSM_STATE = 64
D_IN = 4096
D_FF = 5632
N_BUCKETS = 32
MAX_DISTANCE = 128
EPS = 1e-6
LAMBDA_INIT = 0.8 - 0.6 * math.exp(-0.3 * 0)
LOG2_E = math.log2(math.e)
QK_SCALE = QK_HEAD_DIM ** -0.5 * LOG2_E

N_CPLX = N_SSM_GROUPS * SSM_STATE
GROUPS_PER_OCTET = 8
N_OCTETS = N_SSM_GROUPS // GROUPS_PER_OCTET
OCTET_STATES = GROUPS_PER_OCTET * SSM_STATE

ATTN_Q_TILE = 512
ATTN_K_TILE = 256
FAR_DISTANCE = 2 * MAX_DISTANCE + 1
N_BIAS_TILES = -(-(FAR_DISTANCE + ATTN_Q_TILE - 1) // ATTN_K_TILE)
MASK_VALUE = -1e30

VMEM_LIMIT = 56 * 1024 * 1024

F32 = jnp.float32
BF16 = jnp.bfloat16


def _params(sem, flags=None):
    return pltpu.CompilerParams(dimension_semantics=sem, vmem_limit_bytes=VMEM_LIMIT, flags=flags)


def _norm_proj_kernel(x_ref, g_ref, w_ref, o_ref, xn_ref):
    @pl.when(pl.program_id(1) == 0)
    def _():
        x = x_ref[...]
        y = x * lax.rsqrt(jnp.mean(x * x, axis=-1, keepdims=True) + EPS)
        xn_ref[...] = (y * g_ref[...]).astype(BF16)

    o_ref[...] = jnp.dot(xn_ref[...], w_ref[...], preferred_element_type=F32).astype(o_ref.dtype)


def _norm_proj(x2, g, w, tm=1024, tn=1024):
    m, d = x2.shape
    n = w.shape[1]
    return pl.pallas_call(
        _norm_proj_kernel,
        grid=(m // tm, n // tn),
        in_specs=[
            pl.BlockSpec((tm, d), lambda i, j: (i, 0)),
            pl.BlockSpec((1, d), lambda i, j: (0, 0)),
            pl.BlockSpec((d, tn), lambda i, j: (0, j)),
        ],
        out_specs=pl.BlockSpec((tm, tn), lambda i, j: (i, j)),
        out_shape=jax.ShapeDtypeStruct((m, n), BF16),
        scratch_shapes=[pltpu.VMEM((tm, d), BF16)],
        compiler_params=_params(("parallel", "arbitrary")),
        name="norm_in_proj",
    )(x2, g, w)


def _attn_kernel(lam_ref, q_ref, k_ref, v_ref, bias_ref, sub_ref, o_ref,
                 acc_ref, m_ref, s_ref, p_ref, alpha_ref):
    t = ATTN_Q_TILE
    tk = ATTN_K_TILE
    qi = pl.program_id(2)
    last = (t // tk) * (qi + 1) - 1

    q = q_ref[...]
    lane = lax.broadcasted_iota(jnp.int32, q.shape, 1)
    zero = jnp.zeros_like(q)
    q2 = jnp.concatenate([jnp.where(lane < QK_HEAD_DIM, q, zero),
                          jnp.where(lane >= QK_HEAD_DIM, q, zero)], axis=0)

    ones = jnp.ones((tk, V_HEAD_DIM), BF16)

    def scores(kt):
        k = k_ref[pl.ds(pl.multiple_of(kt * tk, tk), tk), :]
        return lax.dot_general(q2, k, (((1,), (1,)), ((), ())), preferred_element_type=F32)

    def v_ext(kt):
        v = v_ref[pl.ds(pl.multiple_of(kt * tk, tk), tk), :]
        return jnp.concatenate([v, ones], axis=1)

    def row_max(s):
        m = jnp.max(jnp.maximum(s[:, :V_HEAD_DIM], s[:, V_HEAD_DIM:]), axis=-1, keepdims=True)
        return jnp.broadcast_to(m, (2 * t, V_HEAD_DIM))

    def store_scores(kt, slot, bias_idx=None):
        s = scores(kt)
        if bias_idx is not None:
            bias = bias_ref[bias_idx]
            s = s + jnp.concatenate([bias, bias], axis=0)
        s_ref[slot] = s
        m_ref[slot] = jnp.maximum(m_ref[1 - slot], row_max(s))

    def softmax_stage(slot):
        m_new = m_ref[slot]
        alpha_ref[slot] = jnp.exp2(m_ref[1 - slot] - m_new)
        p_ref[slot] = jnp.exp2(s_ref[slot] - jnp.concatenate([m_new, m_new], axis=1)).astype(BF16)

    def pv_stage(kt, slot):
        a = alpha_ref[slot]
        acc_ref[...] = (acc_ref[...] * jnp.concatenate([a, a], axis=1)
                        + jnp.dot(p_ref[slot], v_ext(kt), preferred_element_type=F32))

    def step(j, slot, bias_idx=None):
        pv_stage(jnp.maximum(j - 1, 0), 1 - slot)
        softmax_stage(slot)
        store_scores(j + 1, 1 - slot, bias_idx)

    m_ref[1] = jnp.full(m_ref.shape[1:], MASK_VALUE, F32)
    acc_ref[...] = jnp.zeros(acc_ref.shape, F32)
    p_ref[1] = jnp.zeros(p_ref.shape[1:], BF16)
    alpha_ref[1] = jnp.ones(alpha_ref.shape[1:], F32)

    assert N_BIAS_TILES == 3 and t == 2 * tk

    @pl.when(qi == 0)
    def _():
        store_scores(0, 0, bias_idx=1)

    @pl.when(qi > 0)
    def _():
        store_scores(0, 0)

    n_pairs = jnp.maximum(qi - 1, 0)

    def quad_body(i, carry):
        for k in range(4):
            step(4 * i + k, k % 2)
        return carry

    lax.fori_loop(0, n_pairs // 2, quad_body, 0)

    @pl.when(n_pairs % 2 == 1)
    def _():
        step(2 * n_pairs - 2, 0)
        step(2 * n_pairs - 1, 1)

    @pl.when(qi > 0)
    def _():
        step(last - 3, 0, bias_idx=2)
        step(last - 2, 1, bias_idx=1)

    step(last - 1, 0, bias_idx=0)
    pv_stage(last - 1, 0)
    softmax_stage(1)
    pv_stage(last, 1)

    lam_v = lam_ref[...]
    lam = (jnp.exp(jnp.sum(lam_v[0:1] * lam_v[1:2], axis=-1, keepdims=True))
           - jnp.exp(jnp.sum(lam_v[2:3] * lam_v[3:4], axis=-1, keepdims=True))
           + LAMBDA_INIT)
    acc = acc_ref[...]
    out_a = acc[:t, :V_HEAD_DIM] / acc[:t, V_HEAD_DIM:]
    out_b = acc[t:, :V_HEAD_DIM] / acc[t:, V_HEAD_DIM:]
    o = out_a - lam * out_b
    y = o * lax.rsqrt(jnp.mean(o * o, axis=-1, keepdims=True) + EPS)
    o_ref[...] = ((y * sub_ref[...]) * (1.0 - LAMBDA_INIT)).astype(o_ref.dtype)


def _diff_attention(h3, lam_vecs, bias_tiles, sub_norm):
    b, s, _ = h3.shape
    t = ATTN_Q_TILE
    tk = ATTN_K_TILE
    nh = N_ATTN_HEADS
    return pl.pallas_call(
        _attn_kernel,
        grid=(b, nh, s // t),
        in_specs=[
            pl.BlockSpec((4, QK_HEAD_DIM), lambda bi, hi, qi: (0, 0)),
            pl.BlockSpec((None, t, 128), lambda bi, hi, qi: (bi, qi, hi)),
            pl.BlockSpec((None, s, 128), lambda bi, hi, qi: (bi, 0, nh + hi)),
            pl.BlockSpec((None, s, 128), lambda bi, hi, qi: (bi, 0, 2 * nh + hi)),
            pl.BlockSpec((None, N_BIAS_TILES, t, tk), lambda bi, hi, qi: (hi, 0, 0, 0)),
            pl.BlockSpec((1, V_HEAD_DIM), lambda bi, hi, qi: (0, 0)),
        ],
        out_specs=pl.BlockSpec((None, t, 128), lambda bi, hi, qi: (bi, qi, hi)),
        out_shape=jax.ShapeDtypeStruct((b, s, D_ATTN), BF16),
        scratch_shapes=[pltpu.VMEM((2 * t, 2 * V_HEAD_DIM), F32),
                        pltpu.VMEM((2, 2 * t, V_HEAD_DIM), F32),
                        pltpu.VMEM((2, 2 * t, tk), F32),
                        pltpu.VMEM((2, 2 * t, tk), BF16),
                        pltpu.VMEM((2, 2 * t, V_HEAD_DIM), F32)],
        compiler_params=_params(("parallel", "parallel", "arbitrary")),
        name="diff_attention",
    )(lam_vecs, h3, h3, h3, bias_tiles, sub_norm)


def _attention_bias_tiles(rel_bias, s):
    t, tk = ATTN_Q_TILE, ATTN_K_TILE
    nh = rel_bias.shape[1]
    assert _BUCKET_START[-1] < FAR_DISTANCE <= s
    return pl.pallas_call(
        _bias_tiles_kernel,
        grid=(nh,),
        in_specs=[pl.BlockSpec(memory_space=pltpu.SMEM)],
        out_specs=pl.BlockSpec((None, N_BIAS_TILES, t, tk), lambda h: (h, 0, 0, 0)),
        out_shape=jax.ShapeDtypeStruct((nh, N_BIAS_TILES, t, tk), F32),
        compiler_params=_params(("parallel",)),
        name="attention_bias_tiles",
    )(rel_bias)


def _bucket_starts():
    max_exact = N_BUCKETS // 2
    n = np.arange(max_exact, 4 * MAX_DISTANCE, dtype=np.float64)
    x = np.log(n / max_exact) / math.log(MAX_DISTANCE / max_exact) * (N_BUCKETS - max_exact)
    frac = np.abs(x - np.round(x))[1:]
    assert frac[n[1:] < MAX_DISTANCE].min() > 1e-3
    bucket = np.minimum(max_exact + np.floor(x + 1e-9).astype(np.int64), N_BUCKETS - 1)
    starts = list(range(max_exact)) + [int(n[np.argmax(bucket >= b)]) for b in range(max_exact, N_BUCKETS)]
    assert all(b > a for a, b in zip(starts, starts[1:]))
    return tuple(starts)


_BUCKET_START = _bucket_starts()
BIAS_ROWS = 64


def _bias_tiles_kernel(rel_ref, o_ref):
    t, tk = ATTN_Q_TILE, ATTN_K_TILE
    h = pl.program_id(0)
    far_value = rel_ref[N_BUCKETS - 1, h]
    values = [(rel_ref[b, h] - far_value) * LOG2_E for b in range(N_BUCKETS - 1)]
    row = lax.broadcasted_iota(jnp.int32, (BIAS_ROWS, tk), 0)
    col = lax.broadcasted_iota(jnp.int32, (BIAS_ROWS, tk), 1)
    for idx in range(N_BIAS_TILES):
        for r in range(0, t, BIAS_ROWS):
            dist = row - col + (r + tk * idx - (t - tk))
            bias = jnp.zeros((BIAS_ROWS, tk), F32)
            for b in range(N_BUCKETS - 2, -1, -1):
                bias = jnp.where(dist < _BUCKET_START[b + 1], values[b], bias)
            o_ref[idx, r:r + BIAS_ROWS, :] = jnp.where(dist < 0, MASK_VALUE, bias)


N_STATE_TILES = N_CPLX // 128
TILES_PER_OCTET = OCTET_STATES // 128
SCAN_PITCH = 40
SUBLANES = 8


def _s5_kernel(u_ref, bre_ref, bim_ref, cre_ref, cim_ref, lam_ref, d_ref, wglu_ref,
               o_ref, xr_ref, xi_ref, carry_ref):
    tt = u_ref.shape[0]

    @pl.when(pl.program_id(1) == 0)
    def _():
        carry_ref[...] = jnp.zeros_like(carry_ref)

    def time_rows(t0, tile):
        return pl.ds(t0 * SCAN_PITCH + tile, SUBLANES, stride=SCAN_PITCH)

    u = u_ref[...]
    for j in range(N_OCTETS):
        uj = u[:, j * 128:(j + 1) * 128]
        for x_ref, b_ref in ((xr_ref, bre_ref), (xi_ref, bim_ref)):
            bu = jnp.dot(uj, b_ref[j], preferred_element_type=F32)
            for t0 in range(0, tt, SUBLANES):
                for c in range(TILES_PER_OCTET):
                    x_ref[time_rows(t0, TILES_PER_OCTET * j + c), :] = (
                        bu[t0:t0 + SUBLANES, c * 128:(c + 1) * 128])

    lam_r = lam_ref[0]
    lam_i = lam_ref[1]

    def body(t, carry):
        sr, si = carry
        rows = pl.ds(pl.multiple_of(t * SCAN_PITCH, SUBLANES), N_STATE_TILES)
        nr = lam_r * sr - lam_i * si + xr_ref[rows, :]
        ni = lam_r * si + lam_i * sr + xi_ref[rows, :]
        xr_ref[rows, :] = nr
        xi_ref[rows, :] = ni
        return nr, ni

    sr, si = lax.fori_loop(0, tt, body, (carry_ref[0], carry_ref[1]), unroll=2)
    carry_ref[0] = sr
    carry_ref[1] = si

    def natural(x_ref, j):
        return jnp.concatenate(
            [jnp.concatenate([x_ref[time_rows(t0, TILES_PER_OCTET * j + c), :]
                              for c in range(TILES_PER_OCTET)], axis=1)
             for t0 in range(0, tt, SUBLANES)], axis=0)

    ys = []
    for j in range(N_OCTETS):
        ys.append(jnp.dot(natural(xr_ref, j).astype(BF16), cre_ref[j], preferred_element_type=F32)
                  + jnp.dot(natural(xi_ref, j).astype(BF16), cim_ref[j], preferred_element_type=F32))
    y = jnp.concatenate(ys, axis=1) + d_ref[...] * u.astype(F32)
    y = jax.nn.gelu(y).astype(BF16)
    z = jnp.dot(y, wglu_ref[...], preferred_element_type=F32)
    o_ref[...] = (z[:, :D_SSM] * jax.nn.sigmoid(z[:, D_SSM:])).astype(o_ref.dtype)


def _s5_mixer(h3, bre, bim, cre, cim, lam, d_row, w_glu, tt=256):
    b, s, _ = h3.shape
    const3 = lambda bi, ti: (0, 0, 0)
    const2 = lambda bi, ti: (0, 0)
    return pl.pallas_call(
        _s5_kernel,
        grid=(b, s // tt),
        in_specs=[
            pl.BlockSpec((None, tt, D_SSM), lambda bi, ti: (bi, ti, 3)),
            pl.BlockSpec(bre.shape, const3),
            pl.BlockSpec(bim.shape, const3),
            pl.BlockSpec(cre.shape, const3),
            pl.BlockSpec(cim.shape, const3),
            pl.BlockSpec(lam.shape, const3),
            pl.BlockSpec(d_row.shape, const2),
            pl.BlockSpec(w_glu.shape, const2),
        ],
        out_specs=pl.BlockSpec((None, tt, D_SSM), lambda bi, ti: (bi, ti, 0)),
        out_shape=jax.ShapeDtypeStruct((b, s, D_SSM), BF16),
        scratch_shapes=[pltpu.VMEM((tt * SCAN_PITCH, 128), F32),
                        pltpu.VMEM((tt * SCAN_PITCH, 128), F32),
                        pltpu.VMEM((2, N_STATE_TILES, 128), F32)],
        compiler_params=_params(("parallel", "arbitrary")),
        name="s5_mixer",
    )(h3, bre, bim, cre, cim, lam, d_row, w_glu)


def _s5_parameters(a_re, a_im, log_dt, b_re, b_im, c_re, c_im):
    g, p, hc = N_SSM_GROUPS, SSM_STATE, SSM_GROUP
    dt = jnp.exp(log_dt.astype(F32))[:, None]
    ar = a_re.astype(F32)
    ai = a_im.astype(F32)
    mag = jnp.exp(ar * dt)
    lb_re = mag * jnp.cos(ai * dt)
    lb_im = mag * jnp.sin(ai * dt)
    den = ar * ar + ai * ai
    nr, ni = lb_re - 1.0, lb_im
    f_re = (nr * ar + ni * ai) / den
    f_im = (ni * ar - nr * ai) / den
    br = b_re.astype(F32)
    bi = b_im.astype(F32)
    bb_re = f_re[..., None] * br - f_im[..., None] * bi
    bb_im = f_re[..., None] * bi + f_im[..., None] * br

    eye = jnp.eye(GROUPS_PER_OCTET, dtype=F32)

    def b_layout(bb):
        bb4 = bb.reshape(N_OCTETS, GROUPS_PER_OCTET, p, hc)
        return jnp.einsum('jgph,kg->jkhgp', bb4, eye).reshape(
            N_OCTETS, GROUPS_PER_OCTET * hc, OCTET_STATES).astype(BF16)

    def c_layout(cc):
        c4 = cc.astype(F32).reshape(N_OCTETS, GROUPS_PER_OCTET, hc, p)
        return jnp.einsum('jghp,kg->jgpkh', c4, eye).reshape(
            N_OCTETS, OCTET_STATES, GROUPS_PER_OCTET * hc).astype(BF16)

    lam = jnp.stack([lb_re, lb_im]).reshape(2, N_STATE_TILES, 128)
    return b_layout(bb_re), b_layout(bb_im), c_layout(c_re), -c_layout(c_im), lam


def _out_proj_kernel(a_ref, s_ref, w_ref, x_ref, gpost_ref, gpre_ref, x1_ref, xn_ref):
    mix = (jnp.dot(a_ref[...], w_ref[:D_ATTN, :], preferred_element_type=F32)
           + jnp.dot(s_ref[...], w_ref[D_ATTN:, :], preferred_element_type=F32))
    y = mix * lax.rsqrt(jnp.mean(mix * mix, axis=-1, keepdims=True) + EPS)
    x1 = x_ref[...] + y * gpost_ref[...]
    x1_ref[...] = x1
    z = x1 * lax.rsqrt(jnp.mean(x1 * x1, axis=-1, keepdims=True) + EPS)
    xn_ref[...] = (z * gpre_ref[...]).astype(xn_ref.dtype)


def _out_proj(attn2, ssm2, w_out, x2, g_post, g_pre, tm=256):
    m, d = x2.shape
    return pl.pallas_call(
        _out_proj_kernel,
        grid=(m // tm,),
        in_specs=[
            pl.BlockSpec((tm, D_ATTN), lambda i: (i, 0)),
            pl.BlockSpec((tm, D_SSM), lambda i: (i, 0)),
            pl.BlockSpec(w_out.shape, lambda i: (0, 0)),
            pl.BlockSpec((tm, d), lambda i: (i, 0)),
            pl.BlockSpec((1, d), lambda i: (0, 0)),
            pl.BlockSpec((1, d), lambda i: (0, 0)),
        ],
        out_specs=[pl.BlockSpec((tm, d), lambda i: (i, 0)),
                   pl.BlockSpec((tm, d), lambda i: (i, 0))],
        out_shape=[jax.ShapeDtypeStruct((m, d), F32),
                   jax.ShapeDtypeStruct((m, d), BF16)],
        compiler_params=_params(("parallel",)),
        name="out_proj_norm",
    )(attn2, ssm2, w_out, x2, g_post, g_pre)


CONV_HALO = 8
CONV_CHUNK = 256
CONV_ROWS = 64


def _up_conv_kernel(x_ref, wg_ref, wv_ref, cwg_ref, cwv_ref, cbg_ref, cbv_ref, o_ref,
                    h_ref, wgb_ref, wvb_ref, *, tiles_per_seq):
    tm = x_ref.shape[0]
    tn = o_ref.shape[1]
    n_chunks = tn // CONV_CHUNK
    i = pl.program_id(1)
    cur = i % 2
    prev = 1 - cur

    @pl.when(i == 0)
    def _():
        h_ref[...] = jnp.zeros(h_ref.shape, F32)
        wgb_ref[...] = wg_ref[...].astype(BF16)
        wvb_ref[...] = wv_ref[...].astype(BF16)

    seq_start = (i % tiles_per_seq) == 0
    for k in range(2 * n_chunks):
        tail = h_ref[k, prev, tm:tm + CONV_HALO, :]
        h_ref[k, cur, 0:CONV_HALO, :] = jnp.where(seq_start, 0.0, tail)

    x = x_ref[...]

    def matmul(c):
        cols = slice(c * CONV_CHUNK, (c + 1) * CONV_CHUNK)
        for k, w_ref in ((2 * c, wgb_ref), (2 * c + 1, wvb_ref)):
            h_ref[k, cur, CONV_HALO:, :] = jnp.dot(x, w_ref[:, cols], preferred_element_type=F32)

    def conv_gate(c):
        cols = slice(c * CONV_CHUNK, (c + 1) * CONV_CHUNK)

        def conv(k, cw_ref, cb_ref, r):
            rows = lambda shift: slice(CONV_HALO - shift + r, CONV_HALO - shift + r + CONV_ROWS)
            return (cw_ref[0:1, cols] * h_ref[k, prev, rows(2), :]
                    + cw_ref[1:2, cols] * h_ref[k, prev, rows(1), :]
                    + cw_ref[2:3, cols] * h_ref[k, prev, rows(0), :] + cb_ref[:, cols])

        for r in range(0, tm, CONV_ROWS):
            g = conv(2 * c, cwg_ref, cbg_ref, r)
            v = conv(2 * c + 1, cwv_ref, cbv_ref, r)
            o_ref[r:r + CONV_ROWS, cols] = (jax.nn.silu(g) * v).astype(o_ref.dtype)

    for c in range(n_chunks):
        conv_gate(c)
        matmul(c)


def _up_conv(xn2, w_up, conv_w, conv_b, seq_len, tm=512, tn=512):
    m, d = xn2.shape
    nj = D_FF // tn
    ni = m // tm
    kern = functools.partial(_up_conv_kernel, tiles_per_seq=seq_len // tm)
    return pl.pallas_call(
        kern,
        grid=(nj, ni + 1),
        in_specs=[
            pl.BlockSpec((tm, d), lambda j, i: (jnp.minimum(i, ni - 1), 0)),
            pl.BlockSpec((d, tn), lambda j, i: (0, j)),
            pl.BlockSpec((d, tn), lambda j, i: (0, nj + j)),
            pl.BlockSpec((3, tn), lambda j, i: (0, j)),
            pl.BlockSpec((3, tn), lambda j, i: (0, nj + j)),
            pl.BlockSpec((1, tn), lambda j, i: (0, j)),
            pl.BlockSpec((1, tn), lambda j, i: (0, nj + j)),
        ],
        out_specs=pl.BlockSpec((tm, tn), lambda j, i: (jnp.maximum(i - 1, 0), j)),
        out_shape=jax.ShapeDtypeStruct((m, D_FF), BF16),
        scratch_shapes=[pltpu.VMEM((2 * (tn // CONV_CHUNK), 2, tm + CONV_HALO, CONV_CHUNK), F32),
                        pltpu.VMEM((d, tn), BF16),
                        pltpu.VMEM((d, tn), BF16)],
        compiler_params=_params(("parallel", "arbitrary")),
        name="up_conv_gate",
    )(xn2, w_up, w_up, conv_w, conv_w, conv_b, conv_b)


def _down_proj_kernel(a_ref, w_ref, x_ref, g_ref, o_ref, acc_ref):
    k = pl.program_id(1)

    @pl.when(k == 0)
    def _():
        acc_ref[...] = jnp.zeros_like(acc_ref)

    acc_ref[...] += jnp.dot(a_ref[...], w_ref[...], preferred_element_type=F32)

    @pl.when(k == pl.num_programs(1) - 1)
    def _():
        f = acc_ref[...]
        y = f * lax.rsqrt(jnp.mean(f * f, axis=-1, keepdims=True) + EPS)
        o_ref[...] = x_ref[...] + y * g_ref[...]


def _down_proj(gated, w_down, x1, g_post, tm=1024, tk=512):
    m, kdim = gated.shape
    d = w_down.shape[1]
    return pl.pallas_call(
        _down_proj_kernel,
        grid=(m // tm, kdim // tk),
        in_specs=[
            pl.BlockSpec((tm, tk), lambda i, k: (i, k)),
            pl.BlockSpec((tk, d), lambda i, k: (k, 0)),
            pl.BlockSpec((tm, d), lambda i, k: (i, 0)),
            pl.BlockSpec((1, d), lambda i, k: (0, 0)),
        ],
        out_specs=pl.BlockSpec((tm, d), lambda i, k: (i, 0)),
        out_shape=jax.ShapeDtypeStruct((m, d), F32),
        scratch_shapes=[pltpu.VMEM((tm, d), F32)],
        compiler_params=_params(("parallel", "arbitrary")),
        name="down_proj_norm",
    )(gated, w_down, x1, g_post)


def kernel(x, ln_pre_mix, ln_post_mix, ln_pre_ffn, ln_post_ffn, w_in, lam_q1, lam_k1, lam_q2, lam_k2, attn_sub_norm, rel_bias, ssm_a_re, ssm_a_im, ssm_log_dt, ssm_b_re, ssm_b_im, ssm_c_re, ssm_c_im, ssm_d, ssm_w_glu, w_out, w_up, conv_w, conv_b, w_down):
    b, s, d = x.shape
    assert d == D_MODEL and s % ATTN_Q_TILE == 0 and ATTN_Q_TILE % ATTN_K_TILE == 0
    l = 0
    x2 = x.reshape(b * s, d)

    col_scale = jnp.where(jnp.arange(D_IN) < D_QK, QK_SCALE, 1.0).astype(F32)
    h = _norm_proj(x2, ln_pre_mix[l][None, :], (w_in[l] * col_scale[None, :]).astype(BF16))
    h3 = h.reshape(b, s, D_IN)

    lam_vecs = jnp.stack([lam_q1[l], lam_k1[l], lam_q2[l], lam_k2[l]]).astype(F32)
    bias_tiles = _attention_bias_tiles(rel_bias.astype(F32), s)
    attn = _diff_attention(h3, lam_vecs, bias_tiles, attn_sub_norm[l][None, :].astype(F32))

    bre, bim, cre, cim, coef = _s5_parameters(ssm_a_re[l], ssm_a_im[l], ssm_log_dt[l],
                                              ssm_b_re[l], ssm_b_im[l], ssm_c_re[l], ssm_c_im[l])
    ssm = _s5_mixer(h3, bre, bim, cre, cim, coef,
                    ssm_d[l].reshape(1, D_SSM).astype(F32), ssm_w_glu[l].astype(BF16))

    x1, xn2 = _out_proj(attn.reshape(b * s, D_ATTN), ssm.reshape(b * s, D_SSM),
                        w_out[l].astype(BF16), x2, ln_post_mix[l][None, :], ln_pre_ffn[l][None, :])

    gated = _up_conv(xn2, w_up[l].astype(F32), conv_w[l].reshape(3, 2 * D_FF),
                     conv_b[l].reshape(1, 2 * D_FF), s)
    y = _down_proj(gated, w_down[l].astype(BF16), x1, ln_post_ffn[l][None, :])
    return y.reshape(b, s, d)
```

```python
import functools
import math

import numpy as np
import jax
import jax.numpy as jnp
from jax import lax
from jax.experimental import pallas as pl
from jax.experimental.pallas import tpu as pltpu

D_MODEL = 2048
D_ATTN = 1024
D_SSM = 1024
QK_HEAD_DIM = 64
V_HEAD_DIM = 128
N_ATTN_HEADS = 8
D_QK = 1024
SSM_GROUP = 16
N_SSM_GROUPS = 64
SSM_STATE = 64
D_IN = 4096
D_FF = 5632
N_BUCKETS = 32
MAX_DISTANCE = 128
EPS = 1e-6
LAMBDA_INIT = 0.8 - 0.6 * math.exp(-0.3 * 0)
LOG2_E = math.log2(math.e)
QK_SCALE = QK_HEAD_DIM ** -0.5 * LOG2_E

N_CPLX = N_SSM_GROUPS * SSM_STATE
GROUPS_PER_OCTET = 8
N_OCTETS = N_SSM_GROUPS // GROUPS_PER_OCTET
OCTET_STATES = GROUPS_PER_OCTET * SSM_STATE

ATTN_Q_TILE = 512
ATTN_K_TILE = 256
ATTN_HEADS_PER_STEP = 1
FAR_DISTANCE = 2 * MAX_DISTANCE + 1
N_BIAS_TILES = -(-(FAR_DISTANCE + ATTN_Q_TILE - 1) // ATTN_K_TILE)
MASK_VALUE = -1e30

VMEM_LIMIT = 56 * 1024 * 1024

F32 = jnp.float32
BF16 = jnp.bfloat16


def _params(sem, flags=None):
    return pltpu.CompilerParams(dimension_semantics=sem, vmem_limit_bytes=VMEM_LIMIT, flags=flags)


def _norm_proj_kernel(x_ref, g_ref, w_ref, o_ref, xn_ref):
    @pl.when(pl.program_id(1) == 0)
    def _():
        x = x_ref[...]
        y = x * lax.rsqrt(jnp.mean(x * x, axis=-1, keepdims=True) + EPS)
        xn_ref[...] = (y * g_ref[...]).astype(BF16)

    o_ref[...] = jnp.dot(xn_ref[...], w_ref[...], preferred_element_type=F32).astype(o_ref.dtype)


def _norm_proj(x2, g, w, tm=1024, tn=1024):
    m, d = x2.shape
    n = w.shape[1]
    return pl.pallas_call(
        _norm_proj_kernel,
        grid=(m // tm, n // tn),
        in_specs=[
            pl.BlockSpec((tm, d), lambda i, j: (i, 0)),
            pl.BlockSpec((1, d), lambda i, j: (0, 0)),
            pl.BlockSpec((d, tn), lambda i, j: (0, j)),
        ],
        out_specs=pl.BlockSpec((tm, tn), lambda i, j: (i, j)),
        out_shape=jax.ShapeDtypeStruct((m, n), BF16),
        scratch_shapes=[pltpu.VMEM((tm, d), BF16)],
        compiler_params=_params(("parallel", "arbitrary")),
        name="norm_in_proj",
    )(x2, g, w)


def _attn_kernel(lam_ref, q_ref, qn_ref, k_ref, v_ref, bias_ref, sub_ref, o_ref,
                 acc_ref, m_ref, s_ref, p_ref, alpha_ref):
    t = ATTN_Q_TILE
    tk = ATTN_K_TILE
    qi = pl.program_id(2)
    last = (t // tk) * (qi + 1) - 1

    def stack_maps(q):
        lane = lax.broadcasted_iota(jnp.int32, q.shape, 1)
        zero = jnp.zeros_like(q)
        return jnp.concatenate([jnp.where(lane < QK_HEAD_DIM, q, zero),
                                jnp.where(lane >= QK_HEAD_DIM, q, zero)], axis=0)

    heads = range(ATTN_HEADS_PER_STEP)
    head_lanes = lambda hd: slice(hd * 128, (hd + 1) * 128)
    q2 = [stack_maps(q_ref[:, head_lanes(hd)]) for hd in heads]

    ones = jnp.ones((tk, V_HEAD_DIM), BF16)

    def scores(kt, hd):
        k = k_ref[pl.ds(pl.multiple_of(kt * tk, tk), tk), head_lanes(hd)]
        return lax.dot_general(q2[hd], k, (((1,), (1,)), ((), ())), preferred_element_type=F32)

    def v_ext(kt, hd):
        v = v_ref[pl.ds(pl.multiple_of(kt * tk, tk), tk), head_lanes(hd)]
        return jnp.concatenate([v, ones], axis=1)

    def row_max(s):
        m = s[:, :128]
        for c in range(1, tk // 128):
            m = jnp.maximum(m, s[:, c * 128:(c + 1) * 128])
        return jnp.broadcast_to(jnp.max(m, axis=-1, keepdims=True), (2 * t, 128))

    def store_scores(kt, slot, bias_idx=None):
        for hd in heads:
            s = scores(kt, hd)
            if bias_idx is not None:
                bias = bias_ref[hd, bias_idx]
                s = s + jnp.concatenate([bias, bias], axis=0)
            s_ref[hd, slot] = s
            m_ref[hd, slot] = jnp.maximum(m_ref[hd, 1 - slot], row_max(s))

    def softmax_stage(slot):
        for hd in heads:
            m_new = m_ref[hd, slot]
            alpha_ref[hd, slot] = jnp.exp2(m_ref[hd, 1 - slot] - m_new)
            p_ref[hd, slot] = jnp.exp2(
                s_ref[hd, slot] - jnp.concatenate([m_new] * (tk // 128), axis=1)).astype(BF16)

    def pv_stage(kt, slot):
        for hd in heads:
            a = alpha_ref[hd, slot]
            acc_ref[hd] = (acc_ref[hd] * jnp.concatenate([a, a], axis=1)
                           + jnp.dot(p_ref[hd, slot], v_ext(kt, hd), preferred_element_type=F32))

    def step(j, slot, bias_idx=None):
        pv_stage(jnp.maximum(j - 1, 0), 1 - slot)
        softmax_stage(slot)
        store_scores(j + 1, 1 - slot, bias_idx)

    for hd in heads:
        m_ref[hd, 1] = jnp.full(m_ref.shape[2:], MASK_VALUE, F32)
        acc_ref[hd] = jnp.zeros(acc_ref.shape[1:], F32)
        p_ref[hd, 1] = jnp.zeros(p_ref.shape[2:], BF16)
        alpha_ref[hd, 1] = jnp.ones(alpha_ref.shape[2:], F32)

    assert N_BIAS_TILES == 3 and t == 2 * tk

    @pl.when(qi == 0)
    def _():
        store_scores(0, 0, bias_idx=1)

    n_pairs = jnp.maximum(qi - 1, 0)

    def steps(first, count):
        for k in range(count):
            step(first + k, k % 2)

    def oct_body(i, carry):
        steps(8 * i, 8)
        return carry

    lax.fori_loop(0, n_pairs // 4, oct_body, 0)
    done = 8 * (n_pairs // 4)

    @pl.when(n_pairs % 4 >= 2)
    def _():
        steps(done, 4)

    @pl.when(n_pairs % 2 == 1)
    def _():
        steps(2 * n_pairs - 2, 2)

    def drain():
        step(last - 1, 0, bias_idx=0)
        pv_stage(last - 1, 0)
        softmax_stage(1)
        for hd in heads:
            s_next = lax.dot_general(stack_maps(qn_ref[:, head_lanes(hd)]), k_ref[0:tk, head_lanes(hd)],
                                     (((1,), (1,)), ((), ())), preferred_element_type=F32)
            s_ref[hd, 0] = s_next
            m_ref[hd, 0] = row_max(s_next)
        pv_stage(last, 1)

    @pl.when(qi > 0)
    def _():
        step(last - 3, 0, bias_idx=2)
        step(last - 2, 1, bias_idx=1)
        drain()

    @pl.when(qi == 0)
    def _():
        drain()

    lam_v = lam_ref[...]
    lam = (jnp.exp(jnp.sum(lam_v[0:1] * lam_v[1:2], axis=-1, keepdims=True))
           - jnp.exp(jnp.sum(lam_v[2:3] * lam_v[3:4], axis=-1, keepdims=True))
           + LAMBDA_INIT)
    for hd in heads:
        acc = acc_ref[hd]
        out_a = acc[:t, :V_HEAD_DIM] / acc[:t, V_HEAD_DIM:]
        out_b = acc[t:, :V_HEAD_DIM] / acc[t:, V_HEAD_DIM:]
        o = out_a - lam * out_b
        y = o * lax.rsqrt(jnp.mean(o * o, axis=-1, keepdims=True) + EPS)
        o_ref[:, head_lanes(hd)] = ((y * sub_ref[...]) * (1.0 - LAMBDA_INIT)).astype(o_ref.dtype)


def _diff_attention(h3, lam_vecs, bias_tiles, sub_norm):
    b, s, _ = h3.shape
    t = ATTN_Q_TILE
    tk = ATTN_K_TILE
    hps = ATTN_HEADS_PER_STEP
    ng = N_ATTN_HEADS // hps
    return pl.pallas_call(
        _attn_kernel,
        grid=(b, ng, s // t),
        in_specs=[
            pl.BlockSpec((4, QK_HEAD_DIM), lambda bi, hi, qi: (0, 0)),
            pl.BlockSpec((None, t, 128 * hps), lambda bi, hi, qi: (bi, qi, hi)),
            pl.BlockSpec((None, t, 128 * hps),
                         lambda bi, hi, qi: (bi, jnp.minimum(qi + 1, s // t - 1), hi)),
            pl.BlockSpec((None, s, 128 * hps), lambda bi, hi, qi: (bi, 0, ng + hi)),
            pl.BlockSpec((None, s, 128 * hps), lambda bi, hi, qi: (bi, 0, 2 * ng + hi)),
            pl.BlockSpec((hps, N_BIAS_TILES, t, tk), lambda bi, hi, qi: (hi, 0, 0, 0)),
            pl.BlockSpec((1, V_HEAD_DIM), lambda bi, hi, qi: (0, 0)),
        ],
        out_specs=pl.BlockSpec((None, t, 128 * hps), lambda bi, hi, qi: (bi, qi, hi)),
        out_shape=jax.ShapeDtypeStruct((b, s, D_ATTN), BF16),
        scratch_shapes=[pltpu.VMEM((hps, 2 * t, 2 * V_HEAD_DIM), F32),
                        pltpu.VMEM((hps, 2, 2 * t, V_HEAD_DIM), F32),
                        pltpu.VMEM((hps, 2, 2 * t, tk), F32),
                        pltpu.VMEM((hps, 2, 2 * t, tk), BF16),
                        pltpu.VMEM((hps, 2, 2 * t, V_HEAD_DIM), F32)],
        compiler_params=_params(("parallel", "parallel", "arbitrary")),
        name="diff_attention",
    )(lam_vecs, h3, h3, h3, h3, bias_tiles, sub_norm)


def _attention_bias_tiles(rel_bias, s):
    t, tk = ATTN_Q_TILE, ATTN_K_TILE
    nh = rel_bias.shape[1]
    assert _BUCKET_START[-1] < FAR_DISTANCE <= s
    return pl.pallas_call(
        _bias_tiles_kernel,
        grid=(nh,),
        in_specs=[pl.BlockSpec(memory_space=pltpu.SMEM)],
        out_specs=pl.BlockSpec((None, N_BIAS_TILES, t, tk), lambda h: (h, 0, 0, 0)),
        out_shape=jax.ShapeDtypeStruct((nh, N_BIAS_TILES, t, tk), F32),
        compiler_params=_params(("parallel",)),
        name="attention_bias_tiles",
    )(rel_bias)


def _bucket_starts():
    max_exact = N_BUCKETS // 2
    n = np.arange(max_exact, 4 * MAX_DISTANCE, dtype=np.float64)
    x = np.log(n / max_exact) / math.log(MAX_DISTANCE / max_exact) * (N_BUCKETS - max_exact)
    frac = np.abs(x - np.round(x))[1:]
    assert frac[n[1:] < MAX_DISTANCE].min() > 1e-3
    bucket = np.minimum(max_exact + np.floor(x + 1e-9).astype(np.int64), N_BUCKETS - 1)
    starts = list(range(max_exact)) + [int(n[np.argmax(bucket >= b)]) for b in range(max_exact, N_BUCKETS)]
    assert all(b > a for a, b in zip(starts, starts[1:]))
    return tuple(starts)


_BUCKET_START = _bucket_starts()
BIAS_ROWS = 64


def _bias_tiles_kernel(rel_ref, o_ref):
    t, tk = ATTN_Q_TILE, ATTN_K_TILE
    h = pl.program_id(0)
    far_value = rel_ref[N_BUCKETS - 1, h]
    values = [(rel_ref[b, h] - far_value) * LOG2_E for b in range(N_BUCKETS - 1)]
    row = lax.broadcasted_iota(jnp.int32, (BIAS_ROWS, tk), 0)
    col = lax.broadcasted_iota(jnp.int32, (BIAS_ROWS, tk), 1)
    for idx in range(N_BIAS_TILES):
        for r in range(0, t, BIAS_ROWS):
            dist = row - col + (r + tk * idx - (t - tk))
            bias = jnp.zeros((BIAS_ROWS, tk), F32)
            for b in range(N_BUCKETS - 2, -1, -1):
                bias = jnp.where(dist < _BUCKET_START[b + 1], values[b], bias)
            o_ref[idx, r:r + BIAS_ROWS, :] = jnp.where(dist < 0, MASK_VALUE, bias)


N_STATE_TILES = N_CPLX // 128
TILES_PER_OCTET = OCTET_STATES // 128
SCAN_PITCH = 40
SUBLANES = 8


def _s5_kernel(u_ref, bre_ref, bim_ref, cre_ref, cim_ref, lam_ref, d_ref, wglu_ref,
               o_ref, xr_ref, xi_ref, carry_ref):
    tt = u_ref.shape[0]

    @pl.when(pl.program_id(1) == 0)
    def _():
        carry_ref[...] = jnp.zeros_like(carry_ref)

    def time_rows(t0, tile):
        return pl.ds(t0 * SCAN_PITCH + tile, SUBLANES, stride=SCAN_PITCH)

    u = u_ref[...]
    for j in range(N_OCTETS):
        uj = u[:, j * 128:(j + 1) * 128]
        for x_ref, b_ref in ((xr_ref, bre_ref), (xi_ref, bim_ref)):
            bu = jnp.dot(uj, b_ref[j], preferred_element_type=F32)
            for t0 in range(0, tt, SUBLANES):
                for c in range(TILES_PER_OCTET):
                    x_ref[time_rows(t0, TILES_PER_OCTET * j + c), :] = (
                        bu[t0:t0 + SUBLANES, c * 128:(c + 1) * 128])

    lam_r = lam_ref[0]
    lam_i = lam_ref[1]

    def body(t, carry):
        sr, si = carry
        rows = pl.ds(pl.multiple_of(t * SCAN_PITCH, SUBLANES), N_STATE_TILES)
        nr = lam_r * sr - lam_i * si + xr_ref[rows, :]
        ni = lam_r * si + lam_i * sr + xi_ref[rows, :]
        xr_ref[rows, :] = nr
        xi_ref[rows, :] = ni
        return nr, ni

    sr, si = lax.fori_loop(0, tt, body, (carry_ref[0], carry_ref[1]), unroll=2)
    carry_ref[0] = sr
    carry_ref[1] = si

    def natural(x_ref, j):
        return jnp.concatenate(
            [jnp.concatenate([x_ref[time_rows(t0, TILES_PER_OCTET * j + c), :]
                              for c in range(TILES_PER_OCTET)], axis=1)
             for t0 in range(0, tt, SUBLANES)], axis=0)

    ys = []
    for j in range(N_OCTETS):
        ys.append(jnp.dot(natural(xr_ref, j).astype(BF16), cre_ref[j], preferred_element_type=F32)
                  + jnp.dot(natural(xi_ref, j).astype(BF16), cim_ref[j], preferred_element_type=F32))
    y = jnp.concatenate(ys, axis=1) + d_ref[...] * u.astype(F32)
    y = jax.nn.gelu(y).astype(BF16)
    z = jnp.dot(y, wglu_ref[...], preferred_element_type=F32)
    o_ref[...] = (z[:, :D_SSM] * jax.nn.sigmoid(z[:, D_SSM:])).astype(o_ref.dtype)


def _s5_mixer(h3, bre, bim, cre, cim, lam, d_row, w_glu, tt=512):
    b, s, _ = h3.shape
    const3 = lambda bi, ti: (0, 0, 0)
    const2 = lambda bi, ti: (0, 0)
    return pl.pallas_call(
        _s5_kernel,
        grid=(b, s // tt),
        in_specs=[
            pl.BlockSpec((None, tt, D_SSM), lambda bi, ti: (bi, ti, 3)),
            pl.BlockSpec(bre.shape, const3),
            pl.BlockSpec(bim.shape, const3),
            pl.BlockSpec(cre.shape, const3),
            pl.BlockSpec(cim.shape, const3),
            pl.BlockSpec(lam.shape, const3),
            pl.BlockSpec(d_row.shape, const2),
            pl.BlockSpec(w_glu.shape, const2),
        ],
        out_specs=pl.BlockSpec((None, tt, D_SSM), lambda bi, ti: (bi, ti, 0)),
        out_shape=jax.ShapeDtypeStruct((b, s, D_SSM), BF16),
        scratch_shapes=[pltpu.VMEM((tt * SCAN_PITCH, 128), F32),
                        pltpu.VMEM((tt * SCAN_PITCH, 128), F32),
                        pltpu.VMEM((2, N_STATE_TILES, 128), F32)],
        compiler_params=_params(("parallel", "arbitrary")),
        name="s5_mixer",
    )(h3, bre, bim, cre, cim, lam, d_row, w_glu)


def _s5_parameters(a_re, a_im, log_dt, b_re, b_im, c_re, c_im):
    g, p, hc = N_SSM_GROUPS, SSM_STATE, SSM_GROUP
    dt = jnp.exp(log_dt.astype(F32))[:, None]
    ar = a_re.astype(F32)
    ai = a_im.astype(F32)
    mag = jnp.exp(ar * dt)
    lb_re = mag * jnp.cos(ai * dt)
    lb_im = mag * jnp.sin(ai * dt)
    den = ar * ar + ai * ai
    nr, ni = lb_re - 1.0, lb_im
    f_re = (nr * ar + ni * ai) / den
    f_im = (ni * ar - nr * ai) / den
    br = b_re.astype(F32)
    bi = b_im.astype(F32)
    bb_re = f_re[..., None] * br - f_im[..., None] * bi
    bb_im = f_re[..., None] * bi + f_im[..., None] * br

    eye = jnp.eye(GROUPS_PER_OCTET, dtype=F32)

    def b_layout(bb):
        bb4 = bb.reshape(N_OCTETS, GROUPS_PER_OCTET, p, hc)
        return jnp.einsum('jgph,kg->jkhgp', bb4, eye).reshape(
            N_OCTETS, GROUPS_PER_OCTET * hc, OCTET_STATES).astype(BF16)

    def c_layout(cc):
        c4 = cc.astype(F32).reshape(N_OCTETS, GROUPS_PER_OCTET, hc, p)
        return jnp.einsum('jghp,kg->jgpkh', c4, eye).reshape(
            N_OCTETS, OCTET_STATES, GROUPS_PER_OCTET * hc).astype(BF16)

    lam = jnp.stack([lb_re, lb_im]).reshape(2, N_STATE_TILES, 128)
    return b_layout(bb_re), b_layout(bb_im), c_layout(c_re), -c_layout(c_im), lam


def _out_proj_kernel(a_ref, s_ref, w_ref, x_ref, gpost_ref, gpre_ref, x1_ref, xn_ref):
    mix = (jnp.dot(a_ref[...], w_ref[:D_ATTN, :], preferred_element_type=F32)
           + jnp.dot(s_ref[...], w_ref[D_ATTN:, :], preferred_element_type=F32))
    y = mix * lax.rsqrt(jnp.mean(mix * mix, axis=-1, keepdims=True) + EPS)
    x1 = x_ref[...] + y * gpost_ref[...]
    x1_ref[...] = x1
    z = x1 * lax.rsqrt(jnp.mean(x1 * x1, axis=-1, keepdims=True) + EPS)
    xn_ref[...] = (z * gpre_ref[...]).astype(xn_ref.dtype)


def _out_proj(attn2, ssm2, w_out, x2, g_post, g_pre, tm=256):
    m, d = x2.shape
    return pl.pallas_call(
        _out_proj_kernel,
        grid=(m // tm,),
        in_specs=[
            pl.BlockSpec((tm, D_ATTN), lambda i: (i, 0)),
            pl.BlockSpec((tm, D_SSM), lambda i: (i, 0)),
            pl.BlockSpec(w_out.shape, lambda i: (0, 0)),
            pl.BlockSpec((tm, d), lambda i: (i, 0)),
            pl.BlockSpec((1, d), lambda i: (0, 0)),
            pl.BlockSpec((1, d), lambda i: (0, 0)),
        ],
        out_specs=[pl.BlockSpec((tm, d), lambda i: (i, 0)),
                   pl.BlockSpec((tm, d), lambda i: (i, 0))],
        out_shape=[jax.ShapeDtypeStruct((m, d), F32),
                   jax.ShapeDtypeStruct((m, d), BF16)],
        compiler_params=_params(("parallel",)),
        name="out_proj_norm",
    )(attn2, ssm2, w_out, x2, g_post, g_pre)


CONV_HALO = 8
CONV_CHUNK = 256
CONV_ROWS = 64


def _up_conv_kernel(x_ref, wg_ref, wv_ref, cwg_ref, cwv_ref, cbg_ref, cbv_ref, o_ref,
                    h_ref, wgb_ref, wvb_ref, *, tiles_per_seq):
    tm = x_ref.shape[0]
    tn = o_ref.shape[1]
    n_chunks = tn // CONV_CHUNK
    i = pl.program_id(1)
    cur = i % 2
    prev = 1 - cur

    @pl.when(i == 0)
    def _():
        h_ref[...] = jnp.zeros(h_ref.shape, F32)
        wgb_ref[...] = wg_ref[...].astype(BF16)
        wvb_ref[...] = wv_ref[...].astype(BF16)

    seq_start = (i % tiles_per_seq) == 0
    for k in range(2 * n_chunks):
        tail = h_ref[k, prev, tm:tm + CONV_HALO, :]
        h_ref[k, cur, 0:CONV_HALO, :] = jnp.where(seq_start, 0.0, tail)

    x = x_ref[...]

    def matmul(c):
        cols = slice(c * CONV_CHUNK, (c + 1) * CONV_CHUNK)
        for k, w_ref in ((2 * c, wgb_ref), (2 * c + 1, wvb_ref)):
            h_ref[k, cur, CONV_HALO:, :] = jnp.dot(x, w_ref[:, cols], preferred_element_type=F32)

    def conv_gate(c):
        cols = slice(c * CONV_CHUNK, (c + 1) * CONV_CHUNK)

        def conv(k, cw_ref, cb_ref, r):
            rows = lambda shift: slice(CONV_HALO - shift + r, CONV_HALO - shift + r + CONV_ROWS)
            return (cw_ref[0:1, cols] * h_ref[k, prev, rows(2), :]
                    + cw_ref[1:2, cols] * h_ref[k, prev, rows(1), :]
                    + cw_ref[2:3, cols] * h_ref[k, prev, rows(0), :] + cb_ref[:, cols])

        for r in range(0, tm, CONV_ROWS):
            g = conv(2 * c, cwg_ref, cbg_ref, r)
            v = conv(2 * c + 1, cwv_ref, cbv_ref, r)
            o_ref[r:r + CONV_ROWS, cols] = (jax.nn.silu(g) * v).astype(o_ref.dtype)

    for c in range(n_chunks):
        conv_gate(c)
        matmul(c)


def _up_conv(xn2, w_up, conv_w, conv_b, seq_len, tm=512, tn=512):
    m, d = xn2.shape
    nj = D_FF // tn
    ni = m // tm
    kern = functools.partial(_up_conv_kernel, tiles_per_seq=seq_len // tm)
    return pl.pallas_call(
        kern,
        grid=(nj, ni + 1),
        in_specs=[
            pl.BlockSpec((tm, d), lambda j, i: (jnp.minimum(i, ni - 1), 0)),
            pl.BlockSpec((d, tn), lambda j, i: (0, j)),
            pl.BlockSpec((d, tn), lambda j, i: (0, nj + j)),
            pl.BlockSpec((3, tn), lambda j, i: (0, j)),
            pl.BlockSpec((3, tn), lambda j, i: (0, nj + j)),
            pl.BlockSpec((1, tn), lambda j, i: (0, j)),
            pl.BlockSpec((1, tn), lambda j, i: (0, nj + j)),
        ],
        out_specs=pl.BlockSpec((tm, tn), lambda j, i: (jnp.maximum(i - 1, 0), j)),
        out_shape=jax.ShapeDtypeStruct((m, D_FF), BF16),
        scratch_shapes=[pltpu.VMEM((2 * (tn // CONV_CHUNK), 2, tm + CONV_HALO, CONV_CHUNK), F32),
                        pltpu.VMEM((d, tn), BF16),
                        pltpu.VMEM((d, tn), BF16)],
        compiler_params=_params(("parallel", "arbitrary")),
        name="up_conv_gate",
    )(xn2, w_up, w_up, conv_w, conv_w, conv_b, conv_b)


def _down_proj_kernel(a_ref, w_ref, x_ref, g_ref, o_ref, acc_ref):
    k = pl.program_id(1)

    @pl.when(k == 0)
    def _():
        acc_ref[...] = jnp.zeros_like(acc_ref)

    acc_ref[...] += jnp.dot(a_ref[...], w_ref[...], preferred_element_type=F32)

    @pl.when(k == pl.num_programs(1) - 1)
    def _():
        f = acc_ref[...]
        y = f * lax.rsqrt(jnp.mean(f * f, axis=-1, keepdims=True) + EPS)
        o_ref[...] = x_ref[...] + y * g_ref[...]


def _down_proj(gated, w_down, x1, g_post, tm=1024, tk=512):
    m, kdim = gated.shape
    d = w_down.shape[1]
    return pl.pallas_call(
        _down_proj_kernel,
        grid=(m // tm, kdim // tk),
        in_specs=[
            pl.BlockSpec((tm, tk), lambda i, k: (i, k)),
            pl.BlockSpec((tk, d), lambda i, k: (k, 0)),
            pl.BlockSpec((tm, d), lambda i, k: (i, 0)),
            pl.BlockSpec((1, d), lambda i, k: (0, 0)),
        ],
        out_specs=pl.BlockSpec((tm, d), lambda i, k: (i, 0)),
        out_shape=jax.ShapeDtypeStruct((m, d), F32),
        scratch_shapes=[pltpu.VMEM((tm, d), F32)],
        compiler_params=_params(("parallel", "arbitrary")),
        name="down_proj_norm",
    )(gated, w_down, x1, g_post)


def kernel(x, ln_pre_mix, ln_post_mix, ln_pre_ffn, ln_post_ffn, w_in, lam_q1, lam_k1, lam_q2, lam_k2, attn_sub_norm, rel_bias, ssm_a_re, ssm_a_im, ssm_log_dt, ssm_b_re, ssm_b_im, ssm_c_re, ssm_c_im, ssm_d, ssm_w_glu, w_out, w_up, conv_w, conv_b, w_down):
    b, s, d = x.shape
    assert d == D_MODEL and s % ATTN_Q_TILE == 0 and ATTN_Q_TILE % ATTN_K_TILE == 0
    l = 0
    x2 = x.reshape(b * s, d)

    col_scale = jnp.where(jnp.arange(D_IN) < D_QK, QK_SCALE, 1.0).astype(F32)
    h = _norm_proj(x2, ln_pre_mix[l][None, :], (w_in[l] * col_scale[None, :]).astype(BF16))
    h3 = h.reshape(b, s, D_IN)

    lam_vecs = jnp.stack([lam_q1[l], lam_k1[l], lam_q2[l], lam_k2[l]]).astype(F32)
    bias_tiles = _attention_bias_tiles(rel_bias.astype(F32), s)
    attn = _diff_attention(h3, lam_vecs, bias_tiles, attn_sub_norm[l][None, :].astype(F32))

    bre, bim, cre, cim, coef = _s5_parameters(ssm_a_re[l], ssm_a_im[l], ssm_log_dt[l],
                                              ssm_b_re[l], ssm_b_im[l], ssm_c_re[l], ssm_c_im[l])
    ssm = _s5_mixer(h3, bre, bim, cre, cim, coef,
                    ssm_d[l].reshape(1, D_SSM).astype(F32), ssm_w_glu[l].astype(BF16))

    x1, xn2 = _out_proj(attn.reshape(b * s, D_ATTN), ssm.reshape(b * s, D_SSM),
                        w_out[l].astype(BF16), x2, ln_post_mix[l][None, :], ln_pre_ffn[l][None, :])

    gated = _up_conv(xn2, w_up[l].astype(F32), conv_w[l].reshape(3, 2 * D_FF),
                     conv_b[l].reshape(1, 2 * D_FF), s)
    y = _down_proj(gated, w_down[l].astype(BF16), x1, ln_post_ffn[l][None, :])
    return y.reshape(b, s, d)
```

```python
import functools
import math

import numpy as np
import jax
import jax.numpy as jnp
from jax import lax
from jax.experimental import pallas as pl
from jax.experimental.pallas import tpu as pltpu

D_MODEL = 2048
D_ATTN = 1024
D_SSM = 1024
QK_HEAD_DIM = 64
V_HEAD_DIM = 128
N_ATTN_HEADS = 8
D_QK = 1024
SSM_GROUP = 16
N_SSM_GROUPS = 64
SSM_STATE = 64
D_IN = 4096
D_FF = 5632
N_BUCKETS = 32
MAX_DISTANCE = 128
EPS = 1e-6
LAMBDA_INIT = 0.8 - 0.6 * math.exp(-0.3 * 0)
LOG2_E = math.log2(math.e)
QK_SCALE = QK_HEAD_DIM ** -0.5 * LOG2_E

N_CPLX = N_SSM_GROUPS * SSM_STATE
GROUPS_PER_OCTET = 8
N_OCTETS = N_SSM_GROUPS // GROUPS_PER_OCTET
OCTET_STATES = GROUPS_PER_OCTET * SSM_STATE

ATTN_Q_TILE = 512
ATTN_K_TILE = 256
ATTN_HEADS_PER_STEP = 1
FAR_DISTANCE = 2 * MAX_DISTANCE + 1
N_BIAS_TILES = -(-(FAR_DISTANCE + ATTN_Q_TILE - 1) // ATTN_K_TILE)
MASK_VALUE = -1e30

VMEM_LIMIT = 56 * 1024 * 1024

F32 = jnp.float32
BF16 = jnp.bfloat16


def _params(sem, flags=None):
    return pltpu.CompilerParams(dimension_semantics=sem, vmem_limit_bytes=VMEM_LIMIT, flags=flags)


def _norm_proj_kernel(x_ref, g_ref, w_ref, cs_ref, o_ref, xn_ref):
    @pl.when(pl.program_id(1) == 0)
    def _():
        x = x_ref[...]
        y = x * lax.rsqrt(jnp.mean(x * x, axis=-1, keepdims=True) + EPS)
        xn_ref[...] = (y * g_ref[...]).astype(BF16)

    w = (w_ref[...] * cs_ref[...]).astype(BF16)
    o_ref[...] = jnp.dot(xn_ref[...], w, preferred_element_type=F32).astype(o_ref.dtype)


def _norm_proj(x2, g, w, col_scale, tm=1024, tn=1024):
    m, d = x2.shape
    n = w.shape[1]
    return pl.pallas_call(
        _norm_proj_kernel,
        grid=(m // tm, n // tn),
        in_specs=[
            pl.BlockSpec((tm, d), lambda i, j: (i, 0)),
            pl.BlockSpec((1, d), lambda i, j: (0, 0)),
            pl.BlockSpec((d, tn), lambda i, j: (0, j)),
            pl.BlockSpec((1, tn), lambda i, j: (0, j)),
        ],
        out_specs=pl.BlockSpec((tm, tn), lambda i, j: (i, j)),
        out_shape=jax.ShapeDtypeStruct((m, n), BF16),
        scratch_shapes=[pltpu.VMEM((tm, d), BF16)],
        compiler_params=_params(("parallel", "arbitrary")),
        name="norm_in_proj",
    )(x2, g, w, col_scale)


def _attn_kernel(lam_ref, q_ref, qn_ref, k_ref, v_ref, bias_ref, sub_ref, o_ref,
                 acc_ref, m_ref, s_ref, p_ref, alpha_ref):
    t = ATTN_Q_TILE
    tk = ATTN_K_TILE
    qi = pl.program_id(2)
    last = (t // tk) * (qi + 1) - 1

    def stack_maps(q):
        lane = lax.broadcasted_iota(jnp.int32, q.shape, 1)
        zero = jnp.zeros_like(q)
        return jnp.concatenate([jnp.where(lane < QK_HEAD_DIM, q, zero),
                                jnp.where(lane >= QK_HEAD_DIM, q, zero)], axis=0)

    heads = range(ATTN_HEADS_PER_STEP)
    head_lanes = lambda hd: slice(hd * 128, (hd + 1) * 128)
    q2 = [stack_maps(q_ref[:, head_lanes(hd)]) for hd in heads]

    ones = jnp.ones((tk, V_HEAD_DIM), BF16)

    def scores(kt, hd):
        k = k_ref[pl.ds(pl.multiple_of(kt * tk, tk), tk), head_lanes(hd)]
        return lax.dot_general(q2[hd], k, (((1,), (1,)), ((), ())), preferred_element_type=F32)

    def v_ext(kt, hd):
        v = v_ref[pl.ds(pl.multiple_of(kt * tk, tk), tk), head_lanes(hd)]
        return jnp.concatenate([v, ones], axis=1)

    def row_max(s):
        m = s[:, :128]
        for c in range(1, tk // 128):
            m = jnp.maximum(m, s[:, c * 128:(c + 1) * 128])
        return jnp.broadcast_to(jnp.max(m, axis=-1, keepdims=True), (2 * t, 128))

    def store_scores(kt, slot, bias_idx=None):
        for hd in heads:
            s = scores(kt, hd)
            if bias_idx is not None:
                bias = bias_ref[hd, bias_idx]
                s = s + jnp.concatenate([bias, bias], axis=0)
            s_ref[hd, slot] = s
            m_ref[hd, slot] = jnp.maximum(m_ref[hd, 1 - slot], row_max(s))

    def softmax_stage(slot):
        for hd in heads:
            m_new = m_ref[hd, slot]
            alpha_ref[hd, slot] = jnp.exp2(m_ref[hd, 1 - slot] - m_new)
            p_ref[hd, slot] = jnp.exp2(
                s_ref[hd, slot] - jnp.concatenate([m_new] * (tk // 128), axis=1)).astype(BF16)

    def pv_stage(kt, slot):
        for hd in heads:
            a = alpha_ref[hd, slot]
            acc_ref[hd] = (acc_ref[hd] * jnp.concatenate([a, a], axis=1)
                           + jnp.dot(p_ref[hd, slot], v_ext(kt, hd), preferred_element_type=F32))

    def step(j, slot, bias_idx=None):
        pv_stage(jnp.maximum(j - 1, 0), 1 - slot)
        softmax_stage(slot)
        store_scores(j + 1, 1 - slot, bias_idx)

    def arm():
        for hd in heads:
            m_ref[hd, 1] = jnp.full(m_ref.shape[2:], MASK_VALUE, F32)
            acc_ref[hd] = jnp.zeros(acc_ref.shape[1:], F32)
            p_ref[hd, 1] = jnp.zeros(p_ref.shape[2:], BF16)
            alpha_ref[hd, 1] = jnp.ones(alpha_ref.shape[2:], F32)

    lam_v = lam_ref[...]
    lam = (jnp.exp(jnp.sum(lam_v[0:1] * lam_v[1:2], axis=-1, keepdims=True))
           - jnp.exp(jnp.sum(lam_v[2:3] * lam_v[3:4], axis=-1, keepdims=True))
           + LAMBDA_INIT)

    assert N_BIAS_TILES == 3 and t == 2 * tk

    @pl.when(qi == 0)
    def _():
        arm()
        store_scores(0, 0, bias_idx=1)

    n_pairs = jnp.maximum(qi - 1, 0)

    def steps(first, count):
        for k in range(count):
            step(first + k, k % 2)

    def oct_body(i, carry):
        steps(8 * i, 8)
        return carry

    lax.fori_loop(0, n_pairs // 4, oct_body, 0)
    done = 8 * (n_pairs // 4)

    @pl.when(n_pairs % 4 >= 2)
    def _():
        steps(done, 4)

    @pl.when(n_pairs % 2 == 1)
    def _():
        steps(2 * n_pairs - 2, 2)

    def drain():
        step(last - 1, 0, bias_idx=0)
        pv_stage(last - 1, 0)
        softmax_stage(1)
        for hd in heads:
            s_next = lax.dot_general(stack_maps(qn_ref[:, head_lanes(hd)]), k_ref[0:tk, head_lanes(hd)],
                                     (((1,), (1,)), ((), ())), preferred_element_type=F32)
            s_ref[hd, 0] = s_next
            m_ref[hd, 0] = row_max(s_next)
        for hd in heads:
            a = alpha_ref[hd, 1]
            acc = (acc_ref[hd] * jnp.concatenate([a, a], axis=1)
                   + jnp.dot(p_ref[hd, 1], v_ext(last, hd), preferred_element_type=F32))
            out_a = acc[:t, :V_HEAD_DIM] / acc[:t, V_HEAD_DIM:]
            out_b = acc[t:, :V_HEAD_DIM] / acc[t:, V_HEAD_DIM:]
            o = out_a - lam * out_b
            y = o * lax.rsqrt(jnp.mean(o * o, axis=-1, keepdims=True) + EPS)
            o_ref[:, head_lanes(hd)] = ((y * sub_ref[...]) * (1.0 - LAMBDA_INIT)).astype(o_ref.dtype)
        arm()

    @pl.when(qi > 0)
    def _():
        step(last - 3, 0, bias_idx=2)
        step(last - 2, 1, bias_idx=1)
        drain()

    @pl.when(qi == 0)
    def _():
        drain()


def _diff_attention(h3, lam_vecs, bias_tiles, sub_norm):
    b, s, _ = h3.shape
    t = ATTN_Q_TILE
    tk = ATTN_K_TILE
    hps = ATTN_HEADS_PER_STEP
    ng = N_ATTN_HEADS // hps
    return pl.pallas_call(
        _attn_kernel,
        grid=(b, ng, s // t),
        in_specs=[
            pl.BlockSpec((4, QK_HEAD_DIM), lambda bi, hi, qi: (0, 0)),
            pl.BlockSpec((None, t, 128 * hps), lambda bi, hi, qi: (bi, qi, hi)),
            pl.BlockSpec((None, t, 128 * hps),
                         lambda bi, hi, qi: (bi, jnp.minimum(qi + 1, s // t - 1), hi)),
            pl.BlockSpec((None, s, 128 * hps), lambda bi, hi, qi: (bi, 0, ng + hi)),
            pl.BlockSpec((None, s, 128 * hps), lambda bi, hi, qi: (bi, 0, 2 * ng + hi)),
            pl.BlockSpec((hps, N_BIAS_TILES, t, tk), lambda bi, hi, qi: (hi, 0, 0, 0)),
            pl.BlockSpec((1, V_HEAD_DIM), lambda bi, hi, qi: (0, 0)),
        ],
        out_specs=pl.BlockSpec((None, t, 128 * hps), lambda bi, hi, qi: (bi, qi, hi)),
        out_shape=jax.ShapeDtypeStruct((b, s, D_ATTN), BF16),
        scratch_shapes=[pltpu.VMEM((hps, 2 * t, 2 * V_HEAD_DIM), F32),
                        pltpu.VMEM((hps, 2, 2 * t, V_HEAD_DIM), F32),
                        pltpu.VMEM((hps, 2, 2 * t, tk), F32),
                        pltpu.VMEM((hps, 2, 2 * t, tk), BF16),
                        pltpu.VMEM((hps, 2, 2 * t, V_HEAD_DIM), F32)],
        compiler_params=_params(("parallel", "parallel", "arbitrary")),
        name="diff_attention",
    )(lam_vecs, h3, h3, h3, h3, bias_tiles, sub_norm)


def _attention_bias_tiles(rel_bias, s):
    t, tk = ATTN_Q_TILE, ATTN_K_TILE
    nh = rel_bias.shape[1]
    assert _BUCKET_START[-1] < FAR_DISTANCE <= s
    return pl.pallas_call(
        _bias_tiles_kernel,
        grid=(nh,),
        in_specs=[pl.BlockSpec(memory_space=pltpu.SMEM)],
        out_specs=pl.BlockSpec((None, N_BIAS_TILES, t, tk), lambda h: (h, 0, 0, 0)),
        out_shape=jax.ShapeDtypeStruct((nh, N_BIAS_TILES, t, tk), F32),
        compiler_params=_params(("parallel",)),
        name="attention_bias_tiles",
    )(rel_bias)


def _bucket_starts():
    max_exact = N_BUCKETS // 2
    n = np.arange(max_exact, 4 * MAX_DISTANCE, dtype=np.float64)
    x = np.log(n / max_exact) / math.log(MAX_DISTANCE / max_exact) * (N_BUCKETS - max_exact)
    frac = np.abs(x - np.round(x))[1:]
    assert frac[n[1:] < MAX_DISTANCE].min() > 1e-3
    bucket = np.minimum(max_exact + np.floor(x + 1e-9).astype(np.int64), N_BUCKETS - 1)
    starts = list(range(max_exact)) + [int(n[np.argmax(bucket >= b)]) for b in range(max_exact, N_BUCKETS)]
    assert all(b > a for a, b in zip(starts, starts[1:]))
    return tuple(starts)


_BUCKET_START = _bucket_starts()
BIAS_ROWS = 64


def _bias_tiles_kernel(rel_ref, o_ref):
    t, tk = ATTN_Q_TILE, ATTN_K_TILE
    h = pl.program_id(0)
    far_value = rel_ref[N_BUCKETS - 1, h]
    values = [(rel_ref[b, h] - far_value) * LOG2_E for b in range(N_BUCKETS - 1)]
    row = lax.broadcasted_iota(jnp.int32, (BIAS_ROWS, tk), 0)
    col = lax.broadcasted_iota(jnp.int32, (BIAS_ROWS, tk), 1)
    for idx in range(N_BIAS_TILES):
        for r in range(0, t, BIAS_ROWS):
            dist = row - col + (r + tk * idx - (t - tk))
            bias = jnp.zeros((BIAS_ROWS, tk), F32)
            for b in range(N_BUCKETS - 2, -1, -1):
                bias = jnp.where(dist < _BUCKET_START[b + 1], values[b], bias)
            o_ref[idx, r:r + BIAS_ROWS, :] = jnp.where(dist < 0, MASK_VALUE, bias)


N_STATE_TILES = N_CPLX // 128
TILES_PER_OCTET = OCTET_STATES // 128
SCAN_PITCH = 40
SUBLANES = 8


def _s5_kernel(u_ref, bre_ref, bim_ref, cre_ref, cim_ref, lam_ref, d_ref, wglu_ref,
               o_ref, xr_ref, xi_ref, carry_ref):
    tt = u_ref.shape[0]

    @pl.when(pl.program_id(1) == 0)
    def _():
        carry_ref[...] = jnp.zeros_like(carry_ref)

    def time_rows(t0, tile):
        return pl.ds(t0 * SCAN_PITCH + tile, SUBLANES, stride=SCAN_PITCH)

    u = u_ref[...]
    for j in range(N_OCTETS):
        uj = u[:, j * 128:(j + 1) * 128]
        for x_ref, b_ref in ((xr_ref, bre_ref), (xi_ref, bim_ref)):
            bu = jnp.dot(uj, b_ref[j], preferred_element_type=F32)
            for t0 in range(0, tt, SUBLANES):
                for c in range(TILES_PER_OCTET):
                    x_ref[time_rows(t0, TILES_PER_OCTET * j + c), :] = (
                        bu[t0:t0 + SUBLANES, c * 128:(c + 1) * 128])

    lam_r = lam_ref[0]
    lam_i = lam_ref[1]

    def body(t, carry):
        sr, si = carry
        rows = pl.ds(pl.multiple_of(t * SCAN_PITCH, SUBLANES), N_STATE_TILES)
        nr = lam_r * sr - lam_i * si + xr_ref[rows, :]
        ni = lam_r * si + lam_i * sr + xi_ref[rows, :]
        xr_ref[rows, :] = nr
        xi_ref[rows, :] = ni
        return nr, ni

    sr, si = lax.fori_loop(0, tt, body, (carry_ref[0], carry_ref[1]), unroll=2)
    carry_ref[0] = sr
    carry_ref[1] = si

    def natural(x_ref, j):
        return jnp.concatenate(
            [jnp.concatenate([x_ref[time_rows(t0, TILES_PER_OCTET * j + c), :]
                              for c in range(TILES_PER_OCTET)], axis=1)
             for t0 in range(0, tt, SUBLANES)], axis=0)

    ys = []
    for j in range(N_OCTETS):
        ys.append(jnp.dot(natural(xr_ref, j).astype(BF16), cre_ref[j], preferred_element_type=F32)
                  + jnp.dot(natural(xi_ref, j).astype(BF16), cim_ref[j], preferred_element_type=F32))
    y = jnp.concatenate(ys, axis=1) + d_ref[...] * u.astype(F32)
    y = jax.nn.gelu(y).astype(BF16)
    z = jnp.dot(y, wglu_ref[...], preferred_element_type=F32)
    o_ref[...] = (z[:, :D_SSM] * jax.nn.sigmoid(z[:, D_SSM:])).astype(o_ref.dtype)


def _s5_mixer(h3, bre, bim, cre, cim, lam, d_row, w_glu, tt=512):
    b, s, _ = h3.shape
    const3 = lambda bi, ti: (0, 0, 0)
    const2 = lambda bi, ti: (0, 0)
    return pl.pallas_call(
        _s5_kernel,
        grid=(b, s // tt),
        in_specs=[
            pl.BlockSpec((None, tt, D_SSM), lambda bi, ti: (bi, ti, 3)),
            pl.BlockSpec(bre.shape, const3),
            pl.BlockSpec(bim.shape, const3),
            pl.BlockSpec(cre.shape, const3),
            pl.BlockSpec(cim.shape, const3),
            pl.BlockSpec(lam.shape, const3),
            pl.BlockSpec(d_row.shape, const2),
            pl.BlockSpec(w_glu.shape, const2),
        ],
        out_specs=pl.BlockSpec((None, tt, D_SSM), lambda bi, ti: (bi, ti, 0)),
        out_shape=jax.ShapeDtypeStruct((b, s, D_SSM), BF16),
        scratch_shapes=[pltpu.VMEM((tt * SCAN_PITCH, 128), F32),
                        pltpu.VMEM((tt * SCAN_PITCH, 128), F32),
                        pltpu.VMEM((2, N_STATE_TILES, 128), F32)],
        compiler_params=_params(("parallel", "arbitrary")),
        name="s5_mixer",
    )(h3, bre, bim, cre, cim, lam, d_row, w_glu)


def _s5_parameters(a_re, a_im, log_dt, b_re, b_im, c_re, c_im):
    g, p, hc = N_SSM_GROUPS, SSM_STATE, SSM_GROUP
    dt = jnp.exp(log_dt.astype(F32))[:, None]
    ar = a_re.astype(F32)
    ai = a_im.astype(F32)
    mag = jnp.exp(ar * dt)
    lb_re = mag * jnp.cos(ai * dt)
    lb_im = mag * jnp.sin(ai * dt)
    den = ar * ar + ai * ai
    nr, ni = lb_re - 1.0, lb_im
    f_re = (nr * ar + ni * ai) / den
    f_im = (ni * ar - nr * ai) / den
    br = b_re.astype(F32)
    bi = b_im.astype(F32)
    bb_re = f_re[..., None] * br - f_im[..., None] * bi
    bb_im = f_re[..., None] * bi + f_im[..., None] * br

    eye = jnp.eye(GROUPS_PER_OCTET, dtype=F32)

    def b_layout(bb):
        bb4 = bb.reshape(N_OCTETS, GROUPS_PER_OCTET, p, hc)
        return jnp.einsum('jgph,kg->jkhgp', bb4, eye).reshape(
            N_OCTETS, GROUPS_PER_OCTET * hc, OCTET_STATES).astype(BF16)

    def c_layout(cc):
        c4 = cc.astype(F32).reshape(N_OCTETS, GROUPS_PER_OCTET, hc, p)
        return jnp.einsum('jghp,kg->jgpkh', c4, eye).reshape(
            N_OCTETS, OCTET_STATES, GROUPS_PER_OCTET * hc).astype(BF16)

    lam = jnp.stack([lb_re, lb_im]).reshape(2, N_STATE_TILES, 128)
    return b_layout(bb_re), b_layout(bb_im), c_layout(c_re), -c_layout(c_im), lam


def _out_proj_kernel(a_ref, s_ref, w_ref, x_ref, gpost_ref, gpre_ref, x1_ref, xn_ref):
    mix = (jnp.dot(a_ref[...], w_ref[:D_ATTN, :], preferred_element_type=F32)
           + jnp.dot(s_ref[...], w_ref[D_ATTN:, :], preferred_element_type=F32))
    y = mix * lax.rsqrt(jnp.mean(mix * mix, axis=-1, keepdims=True) + EPS)
    x1 = x_ref[...] + y * gpost_ref[...]
    x1_ref[...] = x1
    z = x1 * lax.rsqrt(jnp.mean(x1 * x1, axis=-1, keepdims=True) + EPS)
    xn_ref[...] = (z * gpre_ref[...]).astype(xn_ref.dtype)


def _out_proj(attn2, ssm2, w_out, x2, g_post, g_pre, tm=256):
    m, d = x2.shape
    return pl.pallas_call(
        _out_proj_kernel,
        grid=(m // tm,),
        in_specs=[
            pl.BlockSpec((tm, D_ATTN), lambda i: (i, 0)),
            pl.BlockSpec((tm, D_SSM), lambda i: (i, 0)),
            pl.BlockSpec(w_out.shape, lambda i: (0, 0)),
            pl.BlockSpec((tm, d), lambda i: (i, 0)),
            pl.BlockSpec((1, d), lambda i: (0, 0)),
            pl.BlockSpec((1, d), lambda i: (0, 0)),
        ],
        out_specs=[pl.BlockSpec((tm, d), lambda i: (i, 0)),
                   pl.BlockSpec((tm, d), lambda i: (i, 0))],
        out_shape=[jax.ShapeDtypeStruct((m, d), F32),
                   jax.ShapeDtypeStruct((m, d), BF16)],
        compiler_params=_params(("parallel",)),
        name="out_proj_norm",
    )(attn2, ssm2, w_out, x2, g_post, g_pre)


CONV_HALO = 8
CONV_CHUNK = 256
CONV_ROWS = 64


def _up_conv_kernel(x_ref, wg_ref, wv_ref, cpg_ref, cpv_ref, o_ref,
                    h_ref, wgb_ref, wvb_ref, *, tiles_per_seq):
    tm = x_ref.shape[0]
    tn = o_ref.shape[1]
    n_chunks = tn // CONV_CHUNK
    i = pl.program_id(1)
    cur = i % 2
    prev = 1 - cur

    @pl.when(i == 0)
    def _():
        h_ref[...] = jnp.zeros(h_ref.shape, F32)
        wgb_ref[...] = wg_ref[...].astype(BF16)
        wvb_ref[...] = wv_ref[...].astype(BF16)

    seq_start = (i % tiles_per_seq) == 0
    for k in range(2 * n_chunks):
        tail = h_ref[k, prev, tm:tm + CONV_HALO, :]
        h_ref[k, cur, 0:CONV_HALO, :] = jnp.where(seq_start, 0.0, tail)

    x = x_ref[...]

    def matmul(c):
        cols = slice(c * CONV_CHUNK, (c + 1) * CONV_CHUNK)
        for k, w_ref in ((2 * c, wgb_ref), (2 * c + 1, wvb_ref)):
            h_ref[k, cur, CONV_HALO:, :] = jnp.dot(x, w_ref[:, cols], preferred_element_type=F32)

    def conv_gate(c):
        cols = slice(c * CONV_CHUNK, (c + 1) * CONV_CHUNK)

        def conv(k, cp_ref, r):
            rows = lambda shift: slice(CONV_HALO - shift + r, CONV_HALO - shift + r + CONV_ROWS)
            par = lambda n: jnp.tile(cp_ref[n, :, cols], (CONV_ROWS // SUBLANES, 1))
            return (par(0) * h_ref[k, prev, rows(2), :] + par(1) * h_ref[k, prev, rows(1), :]
                    + par(2) * h_ref[k, prev, rows(0), :] + par(3))

        for r in range(0, tm, CONV_ROWS):
            g = conv(2 * c, cpg_ref, r)
            v = conv(2 * c + 1, cpv_ref, r)
            o_ref[r:r + CONV_ROWS, cols] = (jax.nn.silu(g) * v).astype(o_ref.dtype)

    for c in range(n_chunks):
        conv_gate(c)
        matmul(c)


def _up_conv(xn2, w_up, conv_w, conv_b, seq_len, tm=512, tn=512):
    m, d = xn2.shape
    conv_params = jnp.broadcast_to(jnp.concatenate([conv_w, conv_b], axis=0)[:, None, :],
                                   (4, SUBLANES, conv_w.shape[1]))
    nj = D_FF // tn
    ni = m // tm
    kern = functools.partial(_up_conv_kernel, tiles_per_seq=seq_len // tm)
    return pl.pallas_call(
        kern,
        grid=(nj, ni + 1),
        in_specs=[
            pl.BlockSpec((tm, d), lambda j, i: (jnp.minimum(i, ni - 1), 0)),
            pl.BlockSpec((d, tn), lambda j, i: (0, j)),
            pl.BlockSpec((d, tn), lambda j, i: (0, nj + j)),
            pl.BlockSpec((4, SUBLANES, tn), lambda j, i: (0, 0, j)),
            pl.BlockSpec((4, SUBLANES, tn), lambda j, i: (0, 0, nj + j)),
        ],
        out_specs=pl.BlockSpec((tm, tn), lambda j, i: (jnp.maximum(i - 1, 0), j)),
        out_shape=jax.ShapeDtypeStruct((m, D_FF), BF16),
        scratch_shapes=[pltpu.VMEM((2 * (tn // CONV_CHUNK), 2, tm + CONV_HALO, CONV_CHUNK), F32),
                        pltpu.VMEM((d, tn), BF16),
                        pltpu.VMEM((d, tn), BF16)],
        compiler_params=_params(("parallel", "arbitrary")),
        name="up_conv_gate",
    )(xn2, w_up, w_up, conv_params, conv_params)


def _down_proj_kernel(a_ref, w_ref, x_ref, g_ref, o_ref, acc_ref):
    k = pl.program_id(1)

    @pl.when(k == 0)
    def _():
        acc_ref[...] = jnp.zeros_like(acc_ref)

    acc_ref[...] += jnp.dot(a_ref[...], w_ref[...], preferred_element_type=F32)

    @pl.when(k == pl.num_programs(1) - 1)
    def _():
        f = acc_ref[...]
        y = f * lax.rsqrt(jnp.mean(f * f, axis=-1, keepdims=True) + EPS)
        o_ref[...] = x_ref[...] + y * g_ref[...]


def _down_proj(gated, w_down, x1, g_post, tm=1024, tk=512):
    m, kdim = gated.shape
    d = w_down.shape[1]
    return pl.pallas_call(
        _down_proj_kernel,
        grid=(m // tm, kdim // tk),
        in_specs=[
            pl.BlockSpec((tm, tk), lambda i, k: (i, k)),
            pl.BlockSpec((tk, d), lambda i, k: (k, 0)),
            pl.BlockSpec((tm, d), lambda i, k: (i, 0)),
            pl.BlockSpec((1, d), lambda i, k: (0, 0)),
        ],
        out_specs=pl.BlockSpec((tm, d), lambda i, k: (i, 0)),
        out_shape=jax.ShapeDtypeStruct((m, d), F32),
        scratch_shapes=[pltpu.VMEM((tm, d), F32)],
        compiler_params=_params(("parallel", "arbitrary")),
        name="down_proj_norm",
    )(gated, w_down, x1, g_post)


def kernel(x, ln_pre_mix, ln_post_mix, ln_pre_ffn, ln_post_ffn, w_in, lam_q1, lam_k1, lam_q2, lam_k2, attn_sub_norm, rel_bias, ssm_a_re, ssm_a_im, ssm_log_dt, ssm_b_re, ssm_b_im, ssm_c_re, ssm_c_im, ssm_d, ssm_w_glu, w_out, w_up, conv_w, conv_b, w_down):
    b, s, d = x.shape
    assert d == D_MODEL and s % ATTN_Q_TILE == 0 and ATTN_Q_TILE % ATTN_K_TILE == 0
    l = 0
    x2 = x.reshape(b * s, d)

    col_scale = jnp.where(jnp.arange(D_IN) < D_QK, QK_SCALE, 1.0).astype(F32)
    h = _norm_proj(x2, ln_pre_mix[l][None, :], w_in[l].astype(F32), col_scale[None, :])
    h3 = h.reshape(b, s, D_IN)

    lam_vecs = jnp.stack([lam_q1[l], lam_k1[l], lam_q2[l], lam_k2[l]]).astype(F32)
    bias_tiles = _attention_bias_tiles(rel_bias.astype(F32), s)
    attn = _diff_attention(h3, lam_vecs, bias_tiles, attn_sub_norm[l][None, :].astype(F32))

    bre, bim, cre, cim, coef = _s5_parameters(ssm_a_re[l], ssm_a_im[l], ssm_log_dt[l],
                                              ssm_b_re[l], ssm_b_im[l], ssm_c_re[l], ssm_c_im[l])
    ssm = _s5_mixer(h3, bre, bim, cre, cim, coef,
                    ssm_d[l].reshape(1, D_SSM).astype(F32), ssm_w_glu[l].astype(BF16))

    x1, xn2 = _out_proj(attn.reshape(b * s, D_ATTN), ssm.reshape(b * s, D_SSM),
                        w_out[l].astype(BF16), x2, ln_post_mix[l][None, :], ln_pre_ffn[l][None, :])

    gated = _up_conv(xn2, w_up[l].astype(F32), conv_w[l].reshape(3, 2 * D_FF),
                     conv_b[l].reshape(1, 2 * D_FF), s)
    y = _down_proj(gated, w_down[l].astype(BF16), x1, ln_post_ffn[l][None, :])
    return y.reshape(b, s, d)
```

```python
import functools
import math

import numpy as np
import jax
import jax.numpy as jnp
from jax import lax
from jax.experimental import pallas as pl
from jax.experimental.pallas import tpu as pltpu

D_MODEL = 2048
D_ATTN = 1024
D_SSM = 1024
QK_HEAD_DIM = 64
V_HEAD_DIM = 128
N_ATTN_HEADS = 8
D_QK = 1024
SSM_GROUP = 16
N_SSM_GROUPS = 64
SSM_STATE = 64
D_IN = 4096
D_FF = 5632
N_BUCKETS = 32
MAX_DISTANCE = 128
EPS = 1e-6
LAMBDA_INIT = 0.8 - 0.6 * math.exp(-0.3 * 0)
LOG2_E = math.log2(math.e)
QK_SCALE = QK_HEAD_DIM ** -0.5 * LOG2_E

N_CPLX = N_SSM_GROUPS * SSM_STATE
GROUPS_PER_OCTET = 8
N_OCTETS = N_SSM_GROUPS // GROUPS_PER_OCTET
OCTET_STATES = GROUPS_PER_OCTET * SSM_STATE

ATTN_Q_TILE = 512
ATTN_K_TILE = 256
ATTN_HEADS_PER_STEP = 1
FAR_DISTANCE = 2 * MAX_DISTANCE + 1
N_BIAS_TILES = -(-(FAR_DISTANCE + ATTN_Q_TILE - 1) // ATTN_K_TILE)
MASK_VALUE = -1e30

VMEM_LIMIT = 56 * 1024 * 1024

F32 = jnp.float32
BF16 = jnp.bfloat16


def _params(sem, flags=None):
    return pltpu.CompilerParams(dimension_semantics=sem, vmem_limit_bytes=VMEM_LIMIT, flags=flags)


def _norm_proj_kernel(x_ref, g_ref, w_ref, o_ref, xn_ref):
    @pl.when(pl.program_id(1) == 0)
    def _():
        x = x_ref[...]
        y = x * lax.rsqrt(jnp.mean(x * x, axis=-1, keepdims=True) + EPS)
        xn_ref[...] = (y * g_ref[...]).astype(BF16)

    o_ref[...] = jnp.dot(xn_ref[...], w_ref[...], preferred_element_type=F32).astype(o_ref.dtype)


def _norm_proj(x2, g, w, tm=1024, tn=1024):
    m, d = x2.shape
    n = w.shape[1]
    return pl.pallas_call(
        _norm_proj_kernel,
        grid=(m // tm, n // tn),
        in_specs=[
            pl.BlockSpec((tm, d), lambda i, j: (i, 0)),
            pl.BlockSpec((1, d), lambda i, j: (0, 0)),
            pl.BlockSpec((d, tn), lambda i, j: (0, j)),
        ],
        out_specs=pl.BlockSpec((tm, tn), lambda i, j: (i, j)),
        out_shape=jax.ShapeDtypeStruct((m, n), BF16),
        scratch_shapes=[pltpu.VMEM((tm, d), BF16)],
        compiler_params=_params(("parallel", "arbitrary")),
        name="norm_in_proj",
    )(x2, g, w)


def _attn_kernel(lam_ref, q_ref, qn_ref, k_ref, v_ref, bias_ref, sub_ref, o_ref,
                 acc_ref, m_ref, s_ref, p_ref, alpha_ref):
    t = ATTN_Q_TILE
    tk = ATTN_K_TILE
    qi = pl.program_id(2)
    last = (t // tk) * (qi + 1) - 1

    def stack_maps(q):
        lane = lax.broadcasted_iota(jnp.int32, q.shape, 1)
        zero = jnp.zeros_like(q)
        return jnp.concatenate([jnp.where(lane < QK_HEAD_DIM, q, zero),
                                jnp.where(lane >= QK_HEAD_DIM, q, zero)], axis=0)

    heads = range(ATTN_HEADS_PER_STEP)
    head_lanes = lambda hd: slice(hd * 128, (hd + 1) * 128)
    q2 = [stack_maps(q_ref[:, head_lanes(hd)]) for hd in heads]

    ones = jnp.ones((tk, V_HEAD_DIM), BF16)

    def scores(kt, hd):
        k = k_ref[pl.ds(pl.multiple_of(kt * tk, tk), tk), head_lanes(hd)]
        return lax.dot_general(q2[hd], k, (((1,), (1,)), ((), ())), preferred_element_type=F32)

    def v_ext(kt, hd):
        v = v_ref[pl.ds(pl.multiple_of(kt * tk, tk), tk), head_lanes(hd)]
        return jnp.concatenate([v, ones], axis=1)

    def row_max(s):
        m = s[:, :128]
        for c in range(1, tk // 128):
            m = jnp.maximum(m, s[:, c * 128:(c + 1) * 128])
        return jnp.broadcast_to(jnp.max(m, axis=-1, keepdims=True), (2 * t, 128))

    def store_scores(kt, slot, bias_idx=None):
        for hd in heads:
            s = scores(kt, hd)
            if bias_idx is not None:
                bias = bias_ref[hd, bias_idx]
                s = s + jnp.concatenate([bias, bias], axis=0)
            s_ref[hd, slot] = s
            m_ref[hd, slot] = jnp.maximum(m_ref[hd, 1 - slot], row_max(s))

    def softmax_stage(slot):
        for hd in heads:
            m_new = m_ref[hd, slot]
            alpha_ref[hd, slot] = jnp.exp2(m_ref[hd, 1 - slot] - m_new)
            p_ref[hd, slot] = jnp.exp2(
                s_ref[hd, slot] - jnp.concatenate([m_new] * (tk // 128), axis=1)).astype(BF16)

    def pv_stage(kt, slot):
        for hd in heads:
            a = alpha_ref[hd, slot]
            acc_ref[hd] = (acc_ref[hd] * jnp.concatenate([a, a], axis=1)
                           + jnp.dot(p_ref[hd, slot], v_ext(kt, hd), preferred_element_type=F32))

    def step(j, slot, bias_idx=None):
        pv_stage(jnp.maximum(j - 1, 0), 1 - slot)
        softmax_stage(slot)
        store_scores(j + 1, 1 - slot, bias_idx)

    def arm():
        for hd in heads:
            m_ref[hd, 1] = jnp.full(m_ref.shape[2:], MASK_VALUE, F32)
            acc_ref[hd] = jnp.zeros(acc_ref.shape[1:], F32)
            p_ref[hd, 1] = jnp.zeros(p_ref.shape[2:], BF16)
            alpha_ref[hd, 1] = jnp.ones(alpha_ref.shape[2:], F32)

    lam_v = lam_ref[...]
    lam = (jnp.exp(jnp.sum(lam_v[0:1] * lam_v[1:2], axis=-1, keepdims=True))
           - jnp.exp(jnp.sum(lam_v[2:3] * lam_v[3:4], axis=-1, keepdims=True))
           + LAMBDA_INIT)

    assert N_BIAS_TILES == 3 and t == 2 * tk

    @pl.when(qi == 0)
    def _():
        arm()
        store_scores(0, 0, bias_idx=1)

    n_pairs = jnp.maximum(qi - 1, 0)

    def steps(first, count):
        for k in range(count):
            step(first + k, k % 2)

    def oct_body(i, carry):
        steps(8 * i, 8)
        return carry

    lax.fori_loop(0, n_pairs // 4, oct_body, 0)
    done = 8 * (n_pairs // 4)

    @pl.when(n_pairs % 4 >= 2)
    def _():
        steps(done, 4)

    @pl.when(n_pairs % 2 == 1)
    def _():
        steps(2 * n_pairs - 2, 2)

    def drain():
        step(last - 1, 0, bias_idx=0)
        pv_stage(last - 1, 0)
        softmax_stage(1)
        for hd in heads:
            s_next = lax.dot_general(stack_maps(qn_ref[:, head_lanes(hd)]), k_ref[0:tk, head_lanes(hd)],
                                     (((1,), (1,)), ((), ())), preferred_element_type=F32)
            s_ref[hd, 0] = s_next
            m_ref[hd, 0] = row_max(s_next)
        for hd in heads:
            a = alpha_ref[hd, 1]
            acc = (acc_ref[hd] * jnp.concatenate([a, a], axis=1)
                   + jnp.dot(p_ref[hd, 1], v_ext(last, hd), preferred_element_type=F32))
            out_a = acc[:t, :V_HEAD_DIM] / acc[:t, V_HEAD_DIM:]
            out_b = acc[t:, :V_HEAD_DIM] / acc[t:, V_HEAD_DIM:]
            o = out_a - lam * out_b
            y = o * lax.rsqrt(jnp.mean(o * o, axis=-1, keepdims=True) + EPS)
            o_ref[:, head_lanes(hd)] = ((y * sub_ref[...]) * (1.0 - LAMBDA_INIT)).astype(o_ref.dtype)
        arm()

    @pl.when(qi > 0)
    def _():
        step(last - 3, 0, bias_idx=2)
        step(last - 2, 1, bias_idx=1)
        drain()

    @pl.when(qi == 0)
    def _():
        drain()


def _diff_attention(h3, lam_vecs, bias_tiles, sub_norm):
    b, s, _ = h3.shape
    t = ATTN_Q_TILE
    tk = ATTN_K_TILE
    hps = ATTN_HEADS_PER_STEP
    ng = N_ATTN_HEADS // hps
    return pl.pallas_call(
        _attn_kernel,
        grid=(b, ng, s // t),
        in_specs=[
            pl.BlockSpec((4, QK_HEAD_DIM), lambda bi, hi, qi: (0, 0)),
            pl.BlockSpec((None, t, 128 * hps), lambda bi, hi, qi: (bi, qi, hi)),
            pl.BlockSpec((None, t, 128 * hps),
                         lambda bi, hi, qi: (bi, jnp.minimum(qi + 1, s // t - 1), hi)),
            pl.BlockSpec((None, s, 128 * hps), lambda bi, hi, qi: (bi, 0, ng + hi)),
            pl.BlockSpec((None, s, 128 * hps), lambda bi, hi, qi: (bi, 0, 2 * ng + hi)),
            pl.BlockSpec((hps, N_BIAS_TILES, t, tk), lambda bi, hi, qi: (hi, 0, 0, 0)),
            pl.BlockSpec((1, V_HEAD_DIM), lambda bi, hi, qi: (0, 0)),
        ],
        out_specs=pl.BlockSpec((None, t, 128 * hps), lambda bi, hi, qi: (bi, qi, hi)),
        out_shape=jax.ShapeDtypeStruct((b, s, D_ATTN), BF16),
        scratch_shapes=[pltpu.VMEM((hps, 2 * t, 2 * V_HEAD_DIM), F32),
                        pltpu.VMEM((hps, 2, 2 * t, V_HEAD_DIM), F32),
                        pltpu.VMEM((hps, 2, 2 * t, tk), F32),
                        pltpu.VMEM((hps, 2, 2 * t, tk), BF16),
                        pltpu.VMEM((hps, 2, 2 * t, V_HEAD_DIM), F32)],
        compiler_params=_params(("parallel", "parallel", "arbitrary")),
        name="diff_attention",
    )(lam_vecs, h3, h3, h3, h3, bias_tiles, sub_norm)


def _attention_bias_tiles(rel_bias, s):
    t, tk = ATTN_Q_TILE, ATTN_K_TILE
    nh = rel_bias.shape[1]
    assert _BUCKET_START[-1] < FAR_DISTANCE <= s
    return pl.pallas_call(
        _bias_tiles_kernel,
        grid=(nh,),
        in_specs=[pl.BlockSpec(memory_space=pltpu.SMEM)],
        out_specs=pl.BlockSpec((None, N_BIAS_TILES, t, tk), lambda h: (h, 0, 0, 0)),
        out_shape=jax.ShapeDtypeStruct((nh, N_BIAS_TILES, t, tk), F32),
        compiler_params=_params(("parallel",)),
        name="attention_bias_tiles",
    )(rel_bias)


def _bucket_starts():
    max_exact = N_BUCKETS // 2
    n = np.arange(max_exact, 4 * MAX_DISTANCE, dtype=np.float64)
    x = np.log(n / max_exact) / math.log(MAX_DISTANCE / max_exact) * (N_BUCKETS - max_exact)
    frac = np.abs(x - np.round(x))[1:]
    assert frac[n[1:] < MAX_DISTANCE].min() > 1e-3
    bucket = np.minimum(max_exact + np.floor(x + 1e-9).astype(np.int64), N_BUCKETS - 1)
    starts = list(range(max_exact)) + [int(n[np.argmax(bucket >= b)]) for b in range(max_exact, N_BUCKETS)]
    assert all(b > a for a, b in zip(starts, starts[1:]))
    return tuple(starts)


_BUCKET_START = _bucket_starts()
BIAS_ROWS = 64


def _bias_tiles_kernel(rel_ref, o_ref):
    t, tk = ATTN_Q_TILE, ATTN_K_TILE
    h = pl.program_id(0)
    far_value = rel_ref[N_BUCKETS - 1, h]
    values = [(rel_ref[b, h] - far_value) * LOG2_E for b in range(N_BUCKETS - 1)]
    row = lax.broadcasted_iota(jnp.int32, (BIAS_ROWS, tk), 0)
    col = lax.broadcasted_iota(jnp.int32, (BIAS_ROWS, tk), 1)
    for idx in range(N_BIAS_TILES):
        for r in range(0, t, BIAS_ROWS):
            dist = row - col + (r + tk * idx - (t - tk))
            bias = jnp.zeros((BIAS_ROWS, tk), F32)
            for b in range(N_BUCKETS - 2, -1, -1):
                bias = jnp.where(dist < _BUCKET_START[b + 1], values[b], bias)
            o_ref[idx, r:r + BIAS_ROWS, :] = jnp.where(dist < 0, MASK_VALUE, bias)


N_STATE_TILES = N_CPLX // 128
TILES_PER_OCTET = OCTET_STATES // 128
SCAN_PITCH = 40
SUBLANES = 8


def _s5_kernel(u_ref, bre_ref, bim_ref, cre_ref, cim_ref, lam_ref, d_ref, wglu_ref,
               o_ref, xr_ref, xi_ref, carry_ref):
    tt = u_ref.shape[0]

    @pl.when(pl.program_id(1) == 0)
    def _():
        carry_ref[...] = jnp.zeros_like(carry_ref)

    def time_rows(t0, tile):
        return pl.ds(t0 * SCAN_PITCH + tile, SUBLANES, stride=SCAN_PITCH)

    u = u_ref[...]
    for j in range(N_OCTETS):
        uj = u[:, j * 128:(j + 1) * 128]
        for x_ref, b_ref in ((xr_ref, bre_ref), (xi_ref, bim_ref)):
            for n0 in range(0, TILES_PER_OCTET, 2):
                bu = jnp.dot(uj, b_ref[j, :, n0 * 128:(n0 + 2) * 128],
                             preferred_element_type=F32)
                for t0 in range(0, tt, SUBLANES):
                    for c in range(2):
                        x_ref[time_rows(t0, TILES_PER_OCTET * j + n0 + c), :] = (
                            bu[t0:t0 + SUBLANES, c * 128:(c + 1) * 128])

    lam_r = lam_ref[0]
    lam_i = lam_ref[1]

    def body(t, carry):
        sr, si = carry
        rows = pl.ds(pl.multiple_of(t * SCAN_PITCH, SUBLANES), N_STATE_TILES)
        nr = lam_r * sr - lam_i * si + xr_ref[rows, :]
        ni = lam_r * si + lam_i * sr + xi_ref[rows, :]
        xr_ref[rows, :] = nr
        xi_ref[rows, :] = ni
        return nr, ni

    sr, si = lax.fori_loop(0, tt, body, (carry_ref[0], carry_ref[1]), unroll=2)
    carry_ref[0] = sr
    carry_ref[1] = si

    def natural(x_ref, tile0):
        return jnp.concatenate(
            [jnp.concatenate([x_ref[time_rows(t0, tile0 + c), :] for c in range(2)], axis=1)
             for t0 in range(0, tt, SUBLANES)], axis=0)

    ys = []
    for j in range(N_OCTETS):
        y = None
        for x_ref, c_ref in ((xr_ref, cre_ref), (xi_ref, cim_ref)):
            for n0 in range(0, TILES_PER_OCTET, 2):
                part = jnp.dot(natural(x_ref, TILES_PER_OCTET * j + n0).astype(BF16),
                               c_ref[j, n0 * 128:(n0 + 2) * 128, :], preferred_element_type=F32)
                y = part if y is None else y + part
        ys.append(y)
    y = jnp.concatenate(ys, axis=1) + d_ref[...] * u.astype(F32)
    y = jax.nn.gelu(y).astype(BF16)
    z = jnp.dot(y, wglu_ref[...], preferred_element_type=F32)
    o_ref[...] = (z[:, :D_SSM] * jax.nn.sigmoid(z[:, D_SSM:])).astype(o_ref.dtype)


def _s5_mixer(h3, bre, bim, cre, cim, lam, d_row, w_glu, tt=512):
    b, s, _ = h3.shape
    const3 = lambda bi, ti: (0, 0, 0)
    const2 = lambda bi, ti: (0, 0)
    return pl.pallas_call(
        _s5_kernel,
        grid=(b, s // tt),
        in_specs=[
            pl.BlockSpec((None, tt, D_SSM), lambda bi, ti: (bi, ti, 3)),
            pl.BlockSpec(bre.shape, const3),
            pl.BlockSpec(bim.shape, const3),
            pl.BlockSpec(cre.shape, const3),
            pl.BlockSpec(cim.shape, const3),
            pl.BlockSpec(lam.shape, const3),
            pl.BlockSpec(d_row.shape, const2),
            pl.BlockSpec(w_glu.shape, const2),
        ],
        out_specs=pl.BlockSpec((None, tt, D_SSM), lambda bi, ti: (bi, ti, 0)),
        out_shape=jax.ShapeDtypeStruct((b, s, D_SSM), BF16),
        scratch_shapes=[pltpu.VMEM((tt * SCAN_PITCH, 128), F32),
                        pltpu.VMEM((tt * SCAN_PITCH, 128), F32),
                        pltpu.VMEM((2, N_STATE_TILES, 128), F32)],
        compiler_params=_params(("parallel", "arbitrary")),
        name="s5_mixer",
    )(h3, bre, bim, cre, cim, lam, d_row, w_glu)


def _s5_parameters(a_re, a_im, log_dt, b_re, b_im, c_re, c_im):
    g, p, hc = N_SSM_GROUPS, SSM_STATE, SSM_GROUP
    dt = jnp.exp(log_dt.astype(F32))[:, None]
    ar = a_re.astype(F32)
    ai = a_im.astype(F32)
    mag = jnp.exp(ar * dt)
    lb_re = mag * jnp.cos(ai * dt)
    lb_im = mag * jnp.sin(ai * dt)
    den = ar * ar + ai * ai
    nr, ni = lb_re - 1.0, lb_im
    f_re = (nr * ar + ni * ai) / den
    f_im = (ni * ar - nr * ai) / den
    br = b_re.astype(F32)
    bi = b_im.astype(F32)
    bb_re = f_re[..., None] * br - f_im[..., None] * bi
    bb_im = f_re[..., None] * bi + f_im[..., None] * br

    eye = jnp.eye(GROUPS_PER_OCTET, dtype=F32)

    def b_layout(bb):
        bb4 = bb.reshape(N_OCTETS, GROUPS_PER_OCTET, p, hc)
        return jnp.einsum('jgph,kg->jkhgp', bb4, eye).reshape(
            N_OCTETS, GROUPS_PER_OCTET * hc, OCTET_STATES).astype(BF16)

    def c_layout(cc):
        c4 = cc.astype(F32).reshape(N_OCTETS, GROUPS_PER_OCTET, hc, p)
        return jnp.einsum('jghp,kg->jgpkh', c4, eye).reshape(
            N_OCTETS, OCTET_STATES, GROUPS_PER_OCTET * hc).astype(BF16)

    lam = jnp.stack([lb_re, lb_im]).reshape(2, N_STATE_TILES, 128)
    return b_layout(bb_re), b_layout(bb_im), c_layout(c_re), -c_layout(c_im), lam


def _out_proj_kernel(a_ref, s_ref, w_ref, x_ref, gpost_ref, gpre_ref, x1_ref, xn_ref):
    mix = (jnp.dot(a_ref[...], w_ref[:D_ATTN, :], preferred_element_type=F32)
           + jnp.dot(s_ref[...], w_ref[D_ATTN:, :], preferred_element_type=F32))
    y = mix * lax.rsqrt(jnp.mean(mix * mix, axis=-1, keepdims=True) + EPS)
    x1 = x_ref[...] + y * gpost_ref[...]
    x1_ref[...] = x1
    z = x1 * lax.rsqrt(jnp.mean(x1 * x1, axis=-1, keepdims=True) + EPS)
    xn_ref[...] = (z * gpre_ref[...]).astype(xn_ref.dtype)


def _out_proj(attn2, ssm2, w_out, x2, g_post, g_pre, tm=256):
    m, d = x2.shape
    return pl.pallas_call(
        _out_proj_kernel,
        grid=(m // tm,),
        in_specs=[
            pl.BlockSpec((tm, D_ATTN), lambda i: (i, 0)),
            pl.BlockSpec((tm, D_SSM), lambda i: (i, 0)),
            pl.BlockSpec(w_out.shape, lambda i: (0, 0)),
            pl.BlockSpec((tm, d), lambda i: (i, 0)),
            pl.BlockSpec((1, d), lambda i: (0, 0)),
            pl.BlockSpec((1, d), lambda i: (0, 0)),
        ],
        out_specs=[pl.BlockSpec((tm, d), lambda i: (i, 0)),
                   pl.BlockSpec((tm, d), lambda i: (i, 0))],
        out_shape=[jax.ShapeDtypeStruct((m, d), F32),
                   jax.ShapeDtypeStruct((m, d), BF16)],
        compiler_params=_params(("parallel",)),
        name="out_proj_norm",
    )(attn2, ssm2, w_out, x2, g_post, g_pre)


CONV_HALO = 8
CONV_CHUNK = 256
CONV_ROWS = 64


def _up_conv_kernel(x_ref, wg_ref, wv_ref, cpg_ref, cpv_ref, o_ref,
                    h_ref, h2_ref, wgb_ref, wvb_ref, *, tiles_per_seq):
    tm = x_ref.shape[0]
    tn = o_ref.shape[1]
    n_chunks = tn // CONV_CHUNK
    i = pl.program_id(1)

    @pl.when(i == 0)
    def _():
        h_ref[...] = jnp.zeros(h_ref.shape, F32)
        h2_ref[...] = jnp.zeros(h2_ref.shape, F32)
        wgb_ref[...] = wg_ref[...].astype(BF16)
        wvb_ref[...] = wv_ref[...].astype(BF16)

    seq_start = (i % tiles_per_seq) == 0

    def body(cur_ref, prev_ref):
        x = x_ref[...]

        def conv_gate(c, r):
            cols = slice(c * CONV_CHUNK, (c + 1) * CONV_CHUNK)

            def conv(k, cp_ref):
                rows = lambda shift: slice(CONV_HALO - shift + r, CONV_HALO - shift + r + CONV_ROWS)
                par = lambda n: jnp.tile(cp_ref[n, :, cols], (CONV_ROWS // SUBLANES, 1))
                return (par(0) * prev_ref[k, rows(2), :] + par(1) * prev_ref[k, rows(1), :]
                        + par(2) * prev_ref[k, rows(0), :] + par(3))

            g = conv(2 * c, cpg_ref)
            v = conv(2 * c + 1, cpv_ref)
            o_ref[r:r + CONV_ROWS, cols] = (jax.nn.silu(g) * v).astype(o_ref.dtype)

        n_slices = tm // CONV_ROWS
        k_slice = x.shape[1] // n_slices
        for c in range(n_chunks):
            cols = slice(c * CONV_CHUNK, (c + 1) * CONV_CHUNK)
            acc_g = acc_v = None
            for s in range(n_slices):
                ks = slice(s * k_slice, (s + 1) * k_slice)
                pg = jnp.dot(x[:, ks], wgb_ref[ks, cols], preferred_element_type=F32)
                pv = jnp.dot(x[:, ks], wvb_ref[ks, cols], preferred_element_type=F32)
                acc_g = pg if acc_g is None else acc_g + pg
                acc_v = pv if acc_v is None else acc_v + pv
                conv_gate(c, s * CONV_ROWS)
            cur_ref[2 * c, CONV_HALO:, :] = acc_g
            cur_ref[2 * c + 1, CONV_HALO:, :] = acc_v
        for k in range(2 * n_chunks):
            tail = prev_ref[k, tm:tm + CONV_HALO, :]
            cur_ref[k, 0:CONV_HALO, :] = jnp.where(seq_start, 0.0, tail)

    @pl.when(i % 2 == 0)
    def _():
        body(h_ref, h2_ref)

    @pl.when(i % 2 == 1)
    def _():
        body(h2_ref, h_ref)


def _up_conv(xn2, w_up, conv_w, conv_b, seq_len, tm=512, tn=512):
    m, d = xn2.shape
    conv_params = jnp.broadcast_to(jnp.concatenate([conv_w, conv_b], axis=0)[:, None, :],
                                   (4, SUBLANES, conv_w.shape[1]))
    nj = D_FF // tn
    ni = m // tm
    kern = functools.partial(_up_conv_kernel, tiles_per_seq=seq_len // tm)
    return pl.pallas_call(
        kern,
        grid=(nj, ni + 1),
        in_specs=[
            pl.BlockSpec((tm, d), lambda j, i: (jnp.minimum(i, ni - 1), 0)),
            pl.BlockSpec((d, tn), lambda j, i: (0, j)),
            pl.BlockSpec((d, tn), lambda j, i: (0, nj + j)),
            pl.BlockSpec((4, SUBLANES, tn), lambda j, i: (0, 0, j)),
            pl.BlockSpec((4, SUBLANES, tn), lambda j, i: (0, 0, nj + j)),
        ],
        out_specs=pl.BlockSpec((tm, tn), lambda j, i: (jnp.maximum(i - 1, 0), j)),
        out_shape=jax.ShapeDtypeStruct((m, D_FF), BF16),
        scratch_shapes=[pltpu.VMEM((2 * (tn // CONV_CHUNK), tm + CONV_HALO, CONV_CHUNK), F32),
                        pltpu.VMEM((2 * (tn // CONV_CHUNK), tm + CONV_HALO, CONV_CHUNK), F32),
                        pltpu.VMEM((d, tn), BF16),
                        pltpu.VMEM((d, tn), BF16)],
        compiler_params=_params(("parallel", "arbitrary")),
        name="up_conv_gate",
    )(xn2, w_up, w_up, conv_params, conv_params)


def _down_proj_kernel(a_ref, w_ref, x_ref, g_ref, o_ref, acc_ref):
    k = pl.program_id(1)

    @pl.when(k == 0)
    def _():
        acc_ref[...] = jnp.zeros_like(acc_ref)

    acc_ref[...] += jnp.dot(a_ref[...], w_ref[...], preferred_element_type=F32)

    @pl.when(k == pl.num_programs(1) - 1)
    def _():
        f = acc_ref[...]
        y = f * lax.rsqrt(jnp.mean(f * f, axis=-1, keepdims=True) + EPS)
        o_ref[...] = x_ref[...] + y * g_ref[...]


def _down_proj(gated, w_down, x1, g_post, tm=1024, tk=512):
    m, kdim = gated.shape
    d = w_down.shape[1]
    return pl.pallas_call(
        _down_proj_kernel,
        grid=(m // tm, kdim // tk),
        in_specs=[
            pl.BlockSpec((tm, tk), lambda i, k: (i, k)),
            pl.BlockSpec((tk, d), lambda i, k: (k, 0)),
            pl.BlockSpec((tm, d), lambda i, k: (i, 0)),
            pl.BlockSpec((1, d), lambda i, k: (0, 0)),
        ],
        out_specs=pl.BlockSpec((tm, d), lambda i, k: (i, 0)),
        out_shape=jax.ShapeDtypeStruct((m, d), F32),
        scratch_shapes=[pltpu.VMEM((tm, d), F32)],
        compiler_params=_params(("parallel", "arbitrary")),
        name="down_proj_norm",
    )(gated, w_down, x1, g_post)


def kernel(x, ln_pre_mix, ln_post_mix, ln_pre_ffn, ln_post_ffn, w_in, lam_q1, lam_k1, lam_q2, lam_k2, attn_sub_norm, rel_bias, ssm_a_re, ssm_a_im, ssm_log_dt, ssm_b_re, ssm_b_im, ssm_c_re, ssm_c_im, ssm_d, ssm_w_glu, w_out, w_up, conv_w, conv_b, w_down):
    b, s, d = x.shape
    assert d == D_MODEL and s % ATTN_Q_TILE == 0 and ATTN_Q_TILE % ATTN_K_TILE == 0
    l = 0
    x2 = x.reshape(b * s, d)

    col_scale = jnp.where(jnp.arange(D_IN) < D_QK, QK_SCALE, 1.0).astype(F32)
    h = _norm_proj(x2, ln_pre_mix[l][None, :], (w_in[l] * col_scale[None, :]).astype(BF16))
    h3 = h.reshape(b, s, D_IN)

    lam_vecs = jnp.stack([lam_q1[l], lam_k1[l], lam_q2[l], lam_k2[l]]).astype(F32)
    bias_tiles = _attention_bias_tiles(rel_bias.astype(F32), s)
    attn = _diff_attention(h3, lam_vecs, bias_tiles, attn_sub_norm[l][None, :].astype(F32))

    bre, bim, cre, cim, coef = _s5_parameters(ssm_a_re[l], ssm_a_im[l], ssm_log_dt[l],
                                              ssm_b_re[l], ssm_b_im[l], ssm_c_re[l], ssm_c_im[l])
    ssm = _s5_mixer(h3, bre, bim, cre, cim, coef,
                    ssm_d[l].reshape(1, D_SSM).astype(F32), ssm_w_glu[l].astype(BF16))

    x1, xn2 = _out_proj(attn.reshape(b * s, D_ATTN), ssm.reshape(b * s, D_SSM),
                        w_out[l].astype(BF16), x2, ln_post_mix[l][None, :], ln_pre_ffn[l][None, :])

    gated = _up_conv(xn2, w_up[l].astype(F32), conv_w[l].reshape(3, 2 * D_FF),
                     conv_b[l].reshape(1, 2 * D_FF), s)
    y = _down_proj(gated, w_down[l].astype(BF16), x1, ln_post_ffn[l][None, :])
    return y.reshape(b, s, d)
```

```python
import functools
import math

import numpy as np
import jax
import jax.numpy as jnp
from jax import lax
from jax.experimental import pallas as pl
from jax.experimental.pallas import tpu as pltpu

D_MODEL = 2048
D_ATTN = 1024
D_SSM = 1024
QK_HEAD_DIM = 64
V_HEAD_DIM = 128
N_ATTN_HEADS = 8
D_QK = 1024
SSM_GROUP = 16
N_SSM_GROUPS = 64
SSM_STATE = 64
D_IN = 4096
D_FF = 5632
N_BUCKETS = 32
MAX_DISTANCE = 128
EPS = 1e-6
LAMBDA_INIT = 0.8 - 0.6 * math.exp(-0.3 * 0)
LOG2_E = math.log2(math.e)
QK_SCALE = QK_HEAD_DIM ** -0.5 * LOG2_E

N_CPLX = N_SSM_GROUPS * SSM_STATE
GROUPS_PER_OCTET = 8
N_OCTETS = N_SSM_GROUPS // GROUPS_PER_OCTET
OCTET_STATES = GROUPS_PER_OCTET * SSM_STATE

ATTN_Q_TILE = 512
ATTN_K_TILE = 256
ATTN_HEADS_PER_STEP = 1
FAR_DISTANCE = 2 * MAX_DISTANCE + 1
N_BIAS_TILES = -(-(FAR_DISTANCE + ATTN_Q_TILE - 1) // ATTN_K_TILE)
MASK_VALUE = -1e30

VMEM_LIMIT = 56 * 1024 * 1024

F32 = jnp.float32
BF16 = jnp.bfloat16


def _params(sem, flags=None):
    return pltpu.CompilerParams(dimension_semantics=sem, vmem_limit_bytes=VMEM_LIMIT, flags=flags)


def _norm_proj_kernel(x_ref, g_ref, w_ref, o_ref, xn_ref):
    @pl.when(pl.program_id(1) == 0)
    def _():
        x = x_ref[...]
        y = x * lax.rsqrt(jnp.mean(x * x, axis=-1, keepdims=True) + EPS)
        xn_ref[...] = (y * g_ref[...]).astype(BF16)

    o_ref[...] = jnp.dot(xn_ref[...], w_ref[...], preferred_element_type=F32).astype(o_ref.dtype)


def _norm_proj(x2, g, w, tm=1024, tn=1024):
    m, d = x2.shape
    n = w.shape[1]
    return pl.pallas_call(
        _norm_proj_kernel,
        grid=(m // tm, n // tn),
        in_specs=[
            pl.BlockSpec((tm, d), lambda i, j: (i, 0)),
            pl.BlockSpec((1, d), lambda i, j: (0, 0)),
            pl.BlockSpec((d, tn), lambda i, j: (0, j)),
        ],
        out_specs=pl.BlockSpec((tm, tn), lambda i, j: (i, j)),
        out_shape=jax.ShapeDtypeStruct((m, n), BF16),
        scratch_shapes=[pltpu.VMEM((tm, d), BF16)],
        compiler_params=_params(("parallel", "arbitrary")),
        name="norm_in_proj",
    )(x2, g, w)


def _attn_kernel(lam_ref, q_ref, qn_ref, k_ref, v_ref, bias_ref, sub_ref, o_ref,
                 acc_ref, m_ref, s_ref, p_ref, alpha_ref):
    t = ATTN_Q_TILE
    tk = ATTN_K_TILE
    qi = pl.program_id(2)
    last = (t // tk) * (qi + 1) - 1

    def stack_maps(q):
        lane = lax.broadcasted_iota(jnp.int32, q.shape, 1)
        zero = jnp.zeros_like(q)
        return jnp.concatenate([jnp.where(lane < QK_HEAD_DIM, q, zero),
                                jnp.where(lane >= QK_HEAD_DIM, q, zero)], axis=0)

    heads = range(ATTN_HEADS_PER_STEP)
    head_lanes = lambda hd: slice(hd * 128, (hd + 1) * 128)
    q2 = [stack_maps(q_ref[:, head_lanes(hd)]) for hd in heads]

    ones = jnp.ones((tk, V_HEAD_DIM), BF16)

    def scores(kt, hd):
        k = k_ref[pl.ds(pl.multiple_of(kt * tk, tk), tk), head_lanes(hd)]
        return lax.dot_general(q2[hd], k, (((1,), (1,)), ((), ())), preferred_element_type=F32)

    def v_ext(kt, hd):
        v = v_ref[pl.ds(pl.multiple_of(kt * tk, tk), tk), head_lanes(hd)]
        return jnp.concatenate([v, ones], axis=1)

    def row_max(s):
        m = s[:, :128]
        for c in range(1, tk // 128):
            m = jnp.maximum(m, s[:, c * 128:(c + 1) * 128])
        return jnp.broadcast_to(jnp.max(m, axis=-1, keepdims=True), (2 * t, 128))

    def store_scores(kt, slot, bias_idx=None):
        for hd in heads:
            s = scores(kt, hd)
            if bias_idx is not None:
                bias = bias_ref[hd, bias_idx]
                s = s + jnp.concatenate([bias, bias], axis=0)
            s_ref[hd, slot] = s
            m_ref[hd, slot] = jnp.maximum(m_ref[hd, 1 - slot], row_max(s))

    def softmax_stage(slot):
        for hd in heads:
            m_new = m_ref[hd, slot]
            alpha_ref[hd, slot] = jnp.exp2(m_ref[hd, 1 - slot] - m_new)
            p_ref[hd, slot] = jnp.exp2(
                s_ref[hd, slot] - jnp.concatenate([m_new] * (tk // 128), axis=1)).astype(BF16)

    def pv_stage(kt, slot):
        for hd in heads:
            a = alpha_ref[hd, slot]
            acc_ref[hd] = (acc_ref[hd] * jnp.concatenate([a, a], axis=1)
                           + jnp.dot(p_ref[hd, slot], v_ext(kt, hd), preferred_element_type=F32))

    def step(j, slot, bias_idx=None):
        pv_stage(jnp.maximum(j - 1, 0), 1 - slot)
        softmax_stage(slot)
        store_scores(j + 1, 1 - slot, bias_idx)

    def arm():
        for hd in heads:
            m_ref[hd, 1] = jnp.full(m_ref.shape[2:], MASK_VALUE, F32)
            acc_ref[hd] = jnp.zeros(acc_ref.shape[1:], F32)
            p_ref[hd, 1] = jnp.zeros(p_ref.shape[2:], BF16)
            alpha_ref[hd, 1] = jnp.ones(alpha_ref.shape[2:], F32)

    lam_v = lam_ref[...]
    lam = (jnp.exp(jnp.sum(lam_v[0:1] * lam_v[1:2], axis=-1, keepdims=True))
           - jnp.exp(jnp.sum(lam_v[2:3] * lam_v[3:4], axis=-1, keepdims=True))
           + LAMBDA_INIT)

    assert N_BIAS_TILES == 3 and t == 2 * tk

    @pl.when(qi == 0)
    def _():
        arm()
        store_scores(0, 0, bias_idx=1)

    n_pairs = jnp.maximum(qi - 1, 0)

    def steps(first, count):
        for k in range(count):
            step(first + k, k % 2)

    def oct_body(i, carry):
        steps(8 * i, 8)
        return carry

    lax.fori_loop(0, n_pairs // 4, oct_body, 0)
    done = 8 * (n_pairs // 4)

    @pl.when(n_pairs % 4 >= 2)
    def _():
        steps(done, 4)

    @pl.when(n_pairs % 2 == 1)
    def _():
        steps(2 * n_pairs - 2, 2)

    def drain():
        step(last - 1, 0, bias_idx=0)
        pv_stage(last - 1, 0)
        softmax_stage(1)
        for hd in heads:
            s_next = lax.dot_general(stack_maps(qn_ref[:, head_lanes(hd)]), k_ref[0:tk, head_lanes(hd)],
                                     (((1,), (1,)), ((), ())), preferred_element_type=F32)
            s_ref[hd, 0] = s_next
            m_ref[hd, 0] = row_max(s_next)
        for hd in heads:
            a = alpha_ref[hd, 1]
            acc = (acc_ref[hd] * jnp.concatenate([a, a], axis=1)
                   + jnp.dot(p_ref[hd, 1], v_ext(last, hd), preferred_element_type=F32))
            out_a = acc[:t, :V_HEAD_DIM] / acc[:t, V_HEAD_DIM:]
            out_b = acc[t:, :V_HEAD_DIM] / acc[t:, V_HEAD_DIM:]
            o = out_a - lam * out_b
            y = o * lax.rsqrt(jnp.mean(o * o, axis=-1, keepdims=True) + EPS)
            o_ref[:, head_lanes(hd)] = ((y * sub_ref[...]) * (1.0 - LAMBDA_INIT)).astype(o_ref.dtype)
        arm()

    @pl.when(qi > 0)
    def _():
        step(last - 3, 0, bias_idx=2)
        step(last - 2, 1, bias_idx=1)
        drain()

    @pl.when(qi == 0)
    def _():
        drain()


def _diff_attention(h3, lam_vecs, bias_tiles, sub_norm):
    b, s, _ = h3.shape
    t = ATTN_Q_TILE
    tk = ATTN_K_TILE
    hps = ATTN_HEADS_PER_STEP
    ng = N_ATTN_HEADS // hps
    return pl.pallas_call(
        _attn_kernel,
        grid=(b, ng, s // t),
        in_specs=[
            pl.BlockSpec((4, QK_HEAD_DIM), lambda bi, hi, qi: (0, 0)),
            pl.BlockSpec((None, t, 128 * hps), lambda bi, hi, qi: (bi, qi, hi)),
            pl.BlockSpec((None, t, 128 * hps),
                         lambda bi, hi, qi: (bi, jnp.minimum(qi + 1, s // t - 1), hi)),
            pl.BlockSpec((None, s, 128 * hps), lambda bi, hi, qi: (bi, 0, ng + hi)),
            pl.BlockSpec((None, s, 128 * hps), lambda bi, hi, qi: (bi, 0, 2 * ng + hi)),
            pl.BlockSpec((hps, N_BIAS_TILES, t, tk), lambda bi, hi, qi: (hi, 0, 0, 0)),
            pl.BlockSpec((1, V_HEAD_DIM), lambda bi, hi, qi: (0, 0)),
        ],
        out_specs=pl.BlockSpec((None, t, 128 * hps), lambda bi, hi, qi: (bi, qi, hi)),
        out_shape=jax.ShapeDtypeStruct((b, s, D_ATTN), BF16),
        scratch_shapes=[pltpu.VMEM((hps, 2 * t, 2 * V_HEAD_DIM), F32),
                        pltpu.VMEM((hps, 2, 2 * t, V_HEAD_DIM), F32),
                        pltpu.VMEM((hps, 2, 2 * t, tk), F32),
                        pltpu.VMEM((hps, 2, 2 * t, tk), BF16),
                        pltpu.VMEM((hps, 2, 2 * t, V_HEAD_DIM), F32)],
        compiler_params=_params(("parallel", "parallel", "arbitrary")),
        name="diff_attention",
    )(lam_vecs, h3, h3, h3, h3, bias_tiles, sub_norm)


def _attention_bias_tiles(rel_bias, s):
    t, tk = ATTN_Q_TILE, ATTN_K_TILE
    nh = rel_bias.shape[1]
    assert _BUCKET_START[-1] < FAR_DISTANCE <= s
    return pl.pallas_call(
        _bias_tiles_kernel,
        grid=(nh,),
        in_specs=[pl.BlockSpec(memory_space=pltpu.SMEM)],
        out_specs=pl.BlockSpec((None, N_BIAS_TILES, t, tk), lambda h: (h, 0, 0, 0)),
        out_shape=jax.ShapeDtypeStruct((nh, N_BIAS_TILES, t, tk), F32),
        compiler_params=_params(("parallel",)),
        name="attention_bias_tiles",
    )(rel_bias)


def _bucket_starts():
    max_exact = N_BUCKETS // 2
    n = np.arange(max_exact, 4 * MAX_DISTANCE, dtype=np.float64)
    x = np.log(n / max_exact) / math.log(MAX_DISTANCE / max_exact) * (N_BUCKETS - max_exact)
    frac = np.abs(x - np.round(x))[1:]
    assert frac[n[1:] < MAX_DISTANCE].min() > 1e-3
    bucket = np.minimum(max_exact + np.floor(x + 1e-9).astype(np.int64), N_BUCKETS - 1)
    starts = list(range(max_exact)) + [int(n[np.argmax(bucket >= b)]) for b in range(max_exact, N_BUCKETS)]
    assert all(b > a for a, b in zip(starts, starts[1:]))
    return tuple(starts)


_BUCKET_START = _bucket_starts()
BIAS_ROWS = 64


def _bias_tiles_kernel(rel_ref, o_ref):
    t, tk = ATTN_Q_TILE, ATTN_K_TILE
    h = pl.program_id(0)
    far_value = rel_ref[N_BUCKETS - 1, h]
    values = [(rel_ref[b, h] - far_value) * LOG2_E for b in range(N_BUCKETS - 1)]
    row = lax.broadcasted_iota(jnp.int32, (BIAS_ROWS, tk), 0)
    col = lax.broadcasted_iota(jnp.int32, (BIAS_ROWS, tk), 1)
    for idx in range(N_BIAS_TILES):
        for r in range(0, t, BIAS_ROWS):
            dist = row - col + (r + tk * idx - (t - tk))
            bias = jnp.zeros((BIAS_ROWS, tk), F32)
            for b in range(N_BUCKETS - 2, -1, -1):
                bias = jnp.where(dist < _BUCKET_START[b + 1], values[b], bias)
            o_ref[idx, r:r + BIAS_ROWS, :] = jnp.where(dist < 0, MASK_VALUE, bias)


N_STATE_TILES = N_CPLX // 128
TILES_PER_OCTET = OCTET_STATES // 128
SCAN_PITCH = 40
SUBLANES = 8
GLU_CHUNK = 256


def _s5_kernel(u_ref, bre_ref, bim_ref, cre_ref, cim_ref, lam_ref, d_ref, wglu_ref,
               o_ref, xr_ref, xi_ref, carry_ref, y_ref):
    tt = u_ref.shape[0]
    ti = pl.program_id(1)
    cur = ti % 2
    prev = 1 - cur

    @pl.when(ti == 0)
    def _():
        carry_ref[...] = jnp.zeros_like(carry_ref)
        y_ref[...] = jnp.zeros(y_ref.shape, BF16)

    def time_rows(t0, tile):
        return pl.ds(t0 * SCAN_PITCH + tile, SUBLANES, stride=SCAN_PITCH)

    u = u_ref[...]
    for j in range(N_OCTETS):
        uj = u[:, j * 128:(j + 1) * 128]
        for x_ref, b_ref in ((xr_ref, bre_ref), (xi_ref, bim_ref)):
            for n0 in range(0, TILES_PER_OCTET, 2):
                bu = jnp.dot(uj, b_ref[j, :, n0 * 128:(n0 + 2) * 128],
                             preferred_element_type=F32)
                for t0 in range(0, tt, SUBLANES):
                    for c in range(2):
                        x_ref[time_rows(t0, TILES_PER_OCTET * j + n0 + c), :] = (
                            bu[t0:t0 + SUBLANES, c * 128:(c + 1) * 128])

    lam_r = lam_ref[0]
    lam_i = lam_ref[1]
    state = [carry_ref[0], carry_ref[1]]

    def scan_steps(first, count):
        for t in range(first, first + count):
            rows = slice(t * SCAN_PITCH, t * SCAN_PITCH + N_STATE_TILES)
            sr, si = state
            nr = lam_r * sr - lam_i * si + xr_ref[rows, :]
            ni = lam_r * si + lam_i * sr + xi_ref[rows, :]
            xr_ref[rows, :] = nr
            xi_ref[rows, :] = ni
            state[0], state[1] = nr, ni

    n_gate_chunks = D_SSM // GLU_CHUNK
    n_k = D_SSM // GLU_CHUNK
    steps_per_gap = tt // (2 * n_gate_chunks * n_k)
    t_done = 0
    for c in range(n_gate_chunks):
        z = []
        for part in range(2):
            cols = slice(part * D_SSM + c * GLU_CHUNK, part * D_SSM + (c + 1) * GLU_CHUNK)
            acc = None
            for ks in range(n_k):
                kk = slice(ks * GLU_CHUNK, (ks + 1) * GLU_CHUNK)
                prod = jnp.dot(y_ref[prev, :, kk], wglu_ref[kk, cols], preferred_element_type=F32)
                acc = prod if acc is None else acc + prod
                scan_steps(t_done, steps_per_gap)
                t_done += steps_per_gap
            z.append(acc)
        o_ref[:, c * GLU_CHUNK:(c + 1) * GLU_CHUNK] = (z[0] * jax.nn.sigmoid(z[1])).astype(o_ref.dtype)
    assert t_done == tt
    carry_ref[0], carry_ref[1] = state

    def natural(x_ref, tile0):
        return jnp.concatenate(
            [jnp.concatenate([x_ref[time_rows(t0, tile0 + c), :] for c in range(2)], axis=1)
             for t0 in range(0, tt, SUBLANES)], axis=0)

    ys = []
    for j in range(N_OCTETS):
        y = None
        for x_ref, c_ref in ((xr_ref, cre_ref), (xi_ref, cim_ref)):
            for n0 in range(0, TILES_PER_OCTET, 2):
                part = jnp.dot(natural(x_ref, TILES_PER_OCTET * j + n0).astype(BF16),
                               c_ref[j, n0 * 128:(n0 + 2) * 128, :], preferred_element_type=F32)
                y = part if y is None else y + part
        ys.append(y)
    y = jnp.concatenate(ys, axis=1) + d_ref[...] * u.astype(F32)
    y_ref[cur] = jax.nn.gelu(y).astype(BF16)


def _s5_mixer(h3, bre, bim, cre, cim, lam, d_row, w_glu, tt=256):
    b, s, _ = h3.shape
    n = s // tt
    const3 = lambda bi, ti: (0, 0, 0)
    const2 = lambda bi, ti: (0, 0)
    return pl.pallas_call(
        _s5_kernel,
        grid=(b, n + 1),
        in_specs=[
            pl.BlockSpec((None, tt, D_SSM), lambda bi, ti: (bi, jnp.minimum(ti, n - 1), 3)),
            pl.BlockSpec(bre.shape, const3),
            pl.BlockSpec(bim.shape, const3),
            pl.BlockSpec(cre.shape, const3),
            pl.BlockSpec(cim.shape, const3),
            pl.BlockSpec(lam.shape, const3),
            pl.BlockSpec(d_row.shape, const2),
            pl.BlockSpec(w_glu.shape, const2),
        ],
        out_specs=pl.BlockSpec((None, tt, D_SSM), lambda bi, ti: (bi, jnp.maximum(ti - 1, 0), 0)),
        out_shape=jax.ShapeDtypeStruct((b, s, D_SSM), BF16),
        scratch_shapes=[pltpu.VMEM((tt * SCAN_PITCH, 128), F32),
                        pltpu.VMEM((tt * SCAN_PITCH, 128), F32),
                        pltpu.VMEM((2, N_STATE_TILES, 128), F32),
                        pltpu.VMEM((2, tt, D_SSM), BF16)],
        compiler_params=_params(("parallel", "arbitrary")),
        name="s5_mixer",
    )(h3, bre, bim, cre, cim, lam, d_row, w_glu)


def _s5_parameters(a_re, a_im, log_dt, b_re, b_im, c_re, c_im):
    g, p, hc = N_SSM_GROUPS, SSM_STATE, SSM_GROUP
    dt = jnp.exp(log_dt.astype(F32))[:, None]
    ar = a_re.astype(F32)
    ai = a_im.astype(F32)
    mag = jnp.exp(ar * dt)
    lb_re = mag * jnp.cos(ai * dt)
    lb_im = mag * jnp.sin(ai * dt)
    den = ar * ar + ai * ai
    nr, ni = lb_re - 1.0, lb_im
    f_re = (nr * ar + ni * ai) / den
    f_im = (ni * ar - nr * ai) / den
    br = b_re.astype(F32)
    bi = b_im.astype(F32)
    bb_re = f_re[..., None] * br - f_im[..., None] * bi
    bb_im = f_re[..., None] * bi + f_im[..., None] * br

    eye = jnp.eye(GROUPS_PER_OCTET, dtype=F32)

    def b_layout(bb):
        bb4 = bb.reshape(N_OCTETS, GROUPS_PER_OCTET, p, hc)
        return jnp.einsum('jgph,kg->jkhgp', bb4, eye).reshape(
            N_OCTETS, GROUPS_PER_OCTET * hc, OCTET_STATES).astype(BF16)

    def c_layout(cc):
        c4 = cc.astype(F32).reshape(N_OCTETS, GROUPS_PER_OCTET, hc, p)
        return jnp.einsum('jghp,kg->jgpkh', c4, eye).reshape(
            N_OCTETS, OCTET_STATES, GROUPS_PER_OCTET * hc).astype(BF16)

    lam = jnp.stack([lb_re, lb_im]).reshape(2, N_STATE_TILES, 128)
    return b_layout(bb_re), b_layout(bb_im), c_layout(c_re), -c_layout(c_im), lam


def _out_proj_kernel(a_ref, s_ref, w_ref, x_ref, gpost_ref, gpre_ref, x1_ref, xn_ref):
    mix = (jnp.dot(a_ref[...], w_ref[:D_ATTN, :], preferred_element_type=F32)
           + jnp.dot(s_ref[...], w_ref[D_ATTN:, :], preferred_element_type=F32))
    y = mix * lax.rsqrt(jnp.mean(mix * mix, axis=-1, keepdims=True) + EPS)
    x1 = x_ref[...] + y * gpost_ref[...]
    x1_ref[...] = x1
    z = x1 * lax.rsqrt(jnp.mean(x1 * x1, axis=-1, keepdims=True) + EPS)
    xn_ref[...] = (z * gpre_ref[...]).astype(xn_ref.dtype)


def _out_proj(attn2, ssm2, w_out, x2, g_post, g_pre, tm=256):
    m, d = x2.shape
    return pl.pallas_call(
        _out_proj_kernel,
        grid=(m // tm,),
        in_specs=[
            pl.BlockSpec((tm, D_ATTN), lambda i: (i, 0)),
            pl.BlockSpec((tm, D_SSM), lambda i: (i, 0)),
            pl.BlockSpec(w_out.shape, lambda i: (0, 0)),
            pl.BlockSpec((tm, d), lambda i: (i, 0)),
            pl.BlockSpec((1, d), lambda i: (0, 0)),
            pl.BlockSpec((1, d), lambda i: (0, 0)),
        ],
        out_specs=[pl.BlockSpec((tm, d), lambda i: (i, 0)),
                   pl.BlockSpec((tm, d), lambda i: (i, 0))],
        out_shape=[jax.ShapeDtypeStruct((m, d), F32),
                   jax.ShapeDtypeStruct((m, d), BF16)],
        compiler_params=_params(("parallel",)),
        name="out_proj_norm",
    )(attn2, ssm2, w_out, x2, g_post, g_pre)


CONV_HALO = 8
CONV_CHUNK = 256
CONV_ROWS = 64


def _up_conv_kernel(x_ref, wg_ref, wv_ref, cpg_ref, cpv_ref, o_ref,
                    h_ref, h2_ref, wgb_ref, wvb_ref, *, tiles_per_seq):
    tm = x_ref.shape[0]
    tn = o_ref.shape[1]
    n_chunks = tn // CONV_CHUNK
    i = pl.program_id(1)

    @pl.when(i == 0)
    def _():
        h_ref[...] = jnp.zeros(h_ref.shape, F32)
        h2_ref[...] = jnp.zeros(h2_ref.shape, F32)
        wgb_ref[...] = wg_ref[...].astype(BF16)
        wvb_ref[...] = wv_ref[...].astype(BF16)

    seq_start = (i % tiles_per_seq) == 0

    def body(cur_ref, prev_ref):
        x = x_ref[...]

        def conv_gate(c, r):
            cols = slice(c * CONV_CHUNK, (c + 1) * CONV_CHUNK)

            def conv(k, cp_ref):
                rows = lambda shift: slice(CONV_HALO - shift + r, CONV_HALO - shift + r + CONV_ROWS)
                par = lambda n: jnp.tile(cp_ref[n, :, cols], (CONV_ROWS // SUBLANES, 1))
                return (par(0) * prev_ref[k, rows(2), :] + par(1) * prev_ref[k, rows(1), :]
                        + par(2) * prev_ref[k, rows(0), :] + par(3))

            g = conv(2 * c, cpg_ref)
            v = conv(2 * c + 1, cpv_ref)
            o_ref[r:r + CONV_ROWS, cols] = (jax.nn.silu(g) * v).astype(o_ref.dtype)

        n_slices = tm // CONV_ROWS
        k_slice = x.shape[1] // n_slices
        for c in range(n_chunks):
            cols = slice(c * CONV_CHUNK, (c + 1) * CONV_CHUNK)
            acc_g = acc_v = None
            for s in range(n_slices):
                ks = slice(s * k_slice, (s + 1) * k_slice)
                pg = jnp.dot(x[:, ks], wgb_ref[ks, cols], preferred_element_type=F32)
                pv = jnp.dot(x[:, ks], wvb_ref[ks, cols], preferred_element_type=F32)
                acc_g = pg if acc_g is None else acc_g + pg
                acc_v = pv if acc_v is None else acc_v + pv
                conv_gate(c, s * CONV_ROWS)
            cur_ref[2 * c, CONV_HALO:, :] = acc_g
            cur_ref[2 * c + 1, CONV_HALO:, :] = acc_v
        for k in range(2 * n_chunks):
            tail = prev_ref[k, tm:tm + CONV_HALO, :]
            cur_ref[k, 0:CONV_HALO, :] = jnp.where(seq_start, 0.0, tail)

    @pl.when(i % 2 == 0)
    def _():
        body(h_ref, h2_ref)

    @pl.when(i % 2 == 1)
    def _():
        body(h2_ref, h_ref)


def _up_conv(xn2, w_up, conv_w, conv_b, seq_len, tm=512, tn=512):
    m, d = xn2.shape
    conv_params = jnp.broadcast_to(jnp.concatenate([conv_w, conv_b], axis=0)[:, None, :],
                                   (4, SUBLANES, conv_w.shape[1]))
    nj = D_FF // tn
    ni = m // tm
    kern = functools.partial(_up_conv_kernel, tiles_per_seq=seq_len // tm)
    return pl.pallas_call(
        kern,
        grid=(nj, ni + 1),
        in_specs=[
            pl.BlockSpec((tm, d), lambda j, i: (jnp.minimum(i, ni - 1), 0)),
            pl.BlockSpec((d, tn), lambda j, i: (0, j)),
            pl.BlockSpec((d, tn), lambda j, i: (0, nj + j)),
            pl.BlockSpec((4, SUBLANES, tn), lambda j, i: (0, 0, j)),
            pl.BlockSpec((4, SUBLANES, tn), lambda j, i: (0, 0, nj + j)),
        ],
        out_specs=pl.BlockSpec((tm, tn), lambda j, i: (jnp.maximum(i - 1, 0), j)),
        out_shape=jax.ShapeDtypeStruct((m, D_FF), BF16),
        scratch_shapes=[pltpu.VMEM((2 * (tn // CONV_CHUNK), tm + CONV_HALO, CONV_CHUNK), F32),
                        pltpu.VMEM((2 * (tn // CONV_CHUNK), tm + CONV_HALO, CONV_CHUNK), F32),
                        pltpu.VMEM((d, tn), BF16),
                        pltpu.VMEM((d, tn), BF16)],
        compiler_params=_params(("parallel", "arbitrary")),
        name="up_conv_gate",
    )(xn2, w_up, w_up, conv_params, conv_params)


def _down_proj_kernel(a_ref, w_ref, x_ref, g_ref, o_ref, acc_ref):
    k = pl.program_id(1)

    @pl.when(k == 0)
    def _():
        acc_ref[...] = jnp.zeros_like(acc_ref)

    acc_ref[...] += jnp.dot(a_ref[...], w_ref[...], preferred_element_type=F32)

    @pl.when(k == pl.num_programs(1) - 1)
    def _():
        f = acc_ref[...]
        y = f * lax.rsqrt(jnp.mean(f * f, axis=-1, keepdims=True) + EPS)
        o_ref[...] = x_ref[...] + y * g_ref[...]


def _down_proj(gated, w_down, x1, g_post, tm=1024, tk=512):
    m, kdim = gated.shape
    d = w_down.shape[1]
    return pl.pallas_call(
        _down_proj_kernel,
        grid=(m // tm, kdim // tk),
        in_specs=[
            pl.BlockSpec((tm, tk), lambda i, k: (i, k)),
            pl.BlockSpec((tk, d), lambda i, k: (k, 0)),
            pl.BlockSpec((tm, d), lambda i, k: (i, 0)),
            pl.BlockSpec((1, d), lambda i, k: (0, 0)),
        ],
        out_specs=pl.BlockSpec((tm, d), lambda i, k: (i, 0)),
        out_shape=jax.ShapeDtypeStruct((m, d), F32),
        scratch_shapes=[pltpu.VMEM((tm, d), F32)],
        compiler_params=_params(("parallel", "arbitrary")),
        name="down_proj_norm",
    )(gated, w_down, x1, g_post)


def kernel(x, ln_pre_mix, ln_post_mix, ln_pre_ffn, ln_post_ffn, w_in, lam_q1, lam_k1, lam_q2, lam_k2, attn_sub_norm, rel_bias, ssm_a_re, ssm_a_im, ssm_log_dt, ssm_b_re, ssm_b_im, ssm_c_re, ssm_c_im, ssm_d, ssm_w_glu, w_out, w_up, conv_w, conv_b, w_down):
    b, s, d = x.shape
    assert d == D_MODEL and s % ATTN_Q_TILE == 0 and ATTN_Q_TILE % ATTN_K_TILE == 0
    l = 0
    x2 = x.reshape(b * s, d)

    col_scale = jnp.where(jnp.arange(D_IN) < D_QK, QK_SCALE, 1.0).astype(F32)
    h = _norm_proj(x2, ln_pre_mix[l][None, :], (w_in[l] * col_scale[None, :]).astype(BF16))
    h3 = h.reshape(b, s, D_IN)

    lam_vecs = jnp.stack([lam_q1[l], lam_k1[l], lam_q2[l], lam_k2[l]]).astype(F32)
    bias_tiles = _attention_bias_tiles(rel_bias.astype(F32), s)
    attn = _diff_attention(h3, lam_vecs, bias_tiles, attn_sub_norm[l][None, :].astype(F32))

    bre, bim, cre, cim, coef = _s5_parameters(ssm_a_re[l], ssm_a_im[l], ssm_log_dt[l],
                                              ssm_b_re[l], ssm_b_im[l], ssm_c_re[l], ssm_c_im[l])
    ssm = _s5_mixer(h3, bre, bim, cre, cim, coef,
                    ssm_d[l].reshape(1, D_SSM).astype(F32), ssm_w_glu[l].astype(BF16))

    x1, xn2 = _out_proj(attn.reshape(b * s, D_ATTN), ssm.reshape(b * s, D_SSM),
                        w_out[l].astype(BF16), x2, ln_post_mix[l][None, :], ln_pre_ffn[l][None, :])

    gated = _up_conv(xn2, w_up[l].astype(F32), conv_w[l].reshape(3, 2 * D_FF),
                     conv_b[l].reshape(1, 2 * D_FF), s)
    y = _down_proj(gated, w_down[l].astype(BF16), x1, ln_post_ffn[l][None, :])
    return y.reshape(b, s, d)
```

```python
import functools
import math

import numpy as np
import jax
import jax.numpy as jnp
from jax import lax
from jax.experimental import pallas as pl
from jax.experimental.pallas import tpu as pltpu

D_MODEL = 2048
D_ATTN = 1024
D_SSM = 1024
QK_HEAD_DIM = 64
V_HEAD_DIM = 128
N_ATTN_HEADS = 8
D_QK = 1024
SSM_GROUP = 16
N_SSM_GROUPS = 64
SSM_STATE = 64
D_IN = 4096
D_FF = 5632
N_BUCKETS = 32
MAX_DISTANCE = 128
EPS = 1e-6
LAMBDA_INIT = 0.8 - 0.6 * math.exp(-0.3 * 0)
LOG2_E = math.log2(math.e)
QK_SCALE = QK_HEAD_DIM ** -0.5 * LOG2_E

N_CPLX = N_SSM_GROUPS * SSM_STATE
GROUPS_PER_OCTET = 8
N_OCTETS = N_SSM_GROUPS // GROUPS_PER_OCTET
OCTET_STATES = GROUPS_PER_OCTET * SSM_STATE

ATTN_Q_TILE = 512
ATTN_K_TILE = 256
ATTN_HEADS_PER_STEP = 1
FAR_DISTANCE = 2 * MAX_DISTANCE + 1
N_BIAS_TILES = -(-(FAR_DISTANCE + ATTN_Q_TILE - 1) // ATTN_K_TILE)
MASK_VALUE = -1e30

VMEM_LIMIT = 56 * 1024 * 1024

F32 = jnp.float32
BF16 = jnp.bfloat16


def _params(sem, flags=None):
    return pltpu.CompilerParams(dimension_semantics=sem, vmem_limit_bytes=VMEM_LIMIT, flags=flags)


def _norm_proj_kernel(x_ref, g_ref, w_ref, o_ref, xn_ref):
    @pl.when(pl.program_id(1) == 0)
    def _():
        x = x_ref[...]
        y = x * lax.rsqrt(jnp.mean(x * x, axis=-1, keepdims=True) + EPS)
        xn_ref[...] = (y * g_ref[...]).astype(BF16)

    o_ref[...] = jnp.dot(xn_ref[...], w_ref[...], preferred_element_type=F32).astype(o_ref.dtype)


def _norm_proj(x2, g, w, tm=1024, tn=1024):
    m, d = x2.shape
    n = w.shape[1]
    return pl.pallas_call(
        _norm_proj_kernel,
        grid=(m // tm, n // tn),
        in_specs=[
            pl.BlockSpec((tm, d), lambda i, j: (i, 0)),
            pl.BlockSpec((1, d), lambda i, j: (0, 0)),
            pl.BlockSpec((d, tn), lambda i, j: (0, j)),
        ],
        out_specs=pl.BlockSpec((tm, tn), lambda i, j: (i, j)),
        out_shape=jax.ShapeDtypeStruct((m, n), BF16),
        scratch_shapes=[pltpu.VMEM((tm, d), BF16)],
        compiler_params=_params(("parallel", "arbitrary")),
        name="norm_in_proj",
    )(x2, g, w)


def _attn_kernel(lam_ref, q_ref, qn_ref, k_ref, v_ref, bias_ref, sub_ref, o_ref,
                 acc_ref, m_ref, s_ref):
    t = ATTN_Q_TILE
    tk = ATTN_K_TILE
    qi = pl.program_id(2)
    last = (t // tk) * (qi + 1) - 1

    def stack_maps(q):
        lane = lax.broadcasted_iota(jnp.int32, q.shape, 1)
        zero = jnp.zeros_like(q)
        return jnp.concatenate([jnp.where(lane < QK_HEAD_DIM, q, zero),
                                jnp.where(lane >= QK_HEAD_DIM, q, zero)], axis=0)

    heads = range(ATTN_HEADS_PER_STEP)
    head_lanes = lambda hd: slice(hd * 128, (hd + 1) * 128)
    q2 = [stack_maps(q_ref[:, head_lanes(hd)]) for hd in heads]

    ones = jnp.ones((tk, V_HEAD_DIM), BF16)

    def scores(kt, hd):
        k = k_ref[pl.ds(pl.multiple_of(kt * tk, tk), tk), head_lanes(hd)]
        return lax.dot_general(q2[hd], k, (((1,), (1,)), ((), ())), preferred_element_type=F32)

    def v_ext(kt, hd):
        v = v_ref[pl.ds(pl.multiple_of(kt * tk, tk), tk), head_lanes(hd)]
        return jnp.concatenate([v, ones], axis=1)

    def row_max(s):
        m = s[:, :128]
        for c in range(1, tk // 128):
            m = jnp.maximum(m, s[:, c * 128:(c + 1) * 128])
        return jnp.broadcast_to(jnp.max(m, axis=-1, keepdims=True), (2 * t, 128))

    def store_scores(kt, slot, bias_idx=None):
        for hd in heads:
            s = scores(kt, hd)
            if bias_idx is not None:
                bias = bias_ref[hd, bias_idx]
                s = s + jnp.concatenate([bias, bias], axis=0)
            s_ref[hd, slot] = s
            m_ref[hd, slot] = jnp.maximum(m_ref[hd, 1 - slot], row_max(s))

    def fused(kt, slot, hd):
        m_new = m_ref[hd, slot]
        a = jnp.exp2(m_ref[hd, 1 - slot] - m_new)
        p = jnp.exp2(s_ref[hd, slot] - jnp.concatenate([m_new] * (tk // 128), axis=1)).astype(BF16)
        return (acc_ref[hd] * jnp.concatenate([a, a], axis=1)
                + jnp.dot(p, v_ext(kt, hd), preferred_element_type=F32))

    def step(j, slot, bias_idx=None):
        for hd in heads:
            acc_ref[hd] = fused(j, slot, hd)
        store_scores(j + 1, 1 - slot, bias_idx)

    def arm():
        for hd in heads:
            m_ref[hd, 1] = jnp.full(m_ref.shape[2:], MASK_VALUE, F32)
            acc_ref[hd] = jnp.zeros(acc_ref.shape[1:], F32)

    lam_v = lam_ref[...]
    lam = (jnp.exp(jnp.sum(lam_v[0:1] * lam_v[1:2], axis=-1, keepdims=True))
           - jnp.exp(jnp.sum(lam_v[2:3] * lam_v[3:4], axis=-1, keepdims=True))
           + LAMBDA_INIT)

    assert N_BIAS_TILES == 3 and t == 2 * tk

    @pl.when(qi == 0)
    def _():
        arm()
        store_scores(0, 0, bias_idx=1)

    n_pairs = jnp.maximum(qi - 1, 0)

    def steps(first, count):
        for k in range(count):
            step(first + k, k % 2)

    def oct_body(i, carry):
        steps(8 * i, 8)
        return carry

    lax.fori_loop(0, n_pairs // 4, oct_body, 0)
    done = 8 * (n_pairs // 4)

    @pl.when(n_pairs % 4 >= 2)
    def _():
        steps(done, 4)

    @pl.when(n_pairs % 2 == 1)
    def _():
        steps(2 * n_pairs - 2, 2)

    def drain():
        step(last - 1, 0, bias_idx=0)
        accs = [fused(last, 1, hd) for hd in heads]
        for hd in heads:
            s_next = lax.dot_general(stack_maps(qn_ref[:, head_lanes(hd)]), k_ref[0:tk, head_lanes(hd)],
                                     (((1,), (1,)), ((), ())), preferred_element_type=F32)
            s_ref[hd, 0] = s_next
            m_ref[hd, 0] = row_max(s_next)
        for hd in heads:
            acc = accs[hd]
            out_a = acc[:t, :V_HEAD_DIM] / acc[:t, V_HEAD_DIM:]
            out_b = acc[t:, :V_HEAD_DIM] / acc[t:, V_HEAD_DIM:]
            o = out_a - lam * out_b
            y = o * lax.rsqrt(jnp.mean(o * o, axis=-1, keepdims=True) + EPS)
            o_ref[:, head_lanes(hd)] = ((y * sub_ref[...]) * (1.0 - LAMBDA_INIT)).astype(o_ref.dtype)
        arm()

    @pl.when(qi > 0)
    def _():
        step(last - 3, 0, bias_idx=2)
        step(last - 2, 1, bias_idx=1)
        drain()

    @pl.when(qi == 0)
    def _():
        drain()


def _diff_attention(h3, lam_vecs, bias_tiles, sub_norm):
    b, s, _ = h3.shape
    t = ATTN_Q_TILE
    tk = ATTN_K_TILE
    hps = ATTN_HEADS_PER_STEP
    ng = N_ATTN_HEADS // hps
    return pl.pallas_call(
        _attn_kernel,
        grid=(b, ng, s // t),
        in_specs=[
            pl.BlockSpec((4, QK_HEAD_DIM), lambda bi, hi, qi: (0, 0)),
            pl.BlockSpec((None, t, 128 * hps), lambda bi, hi, qi: (bi, qi, hi)),
            pl.BlockSpec((None, t, 128 * hps),
                         lambda bi, hi, qi: (bi, jnp.minimum(qi + 1, s // t - 1), hi)),
            pl.BlockSpec((None, s, 128 * hps), lambda bi, hi, qi: (bi, 0, ng + hi)),
            pl.BlockSpec((None, s, 128 * hps), lambda bi, hi, qi: (bi, 0, 2 * ng + hi)),
            pl.BlockSpec((hps, N_BIAS_TILES, t, tk), lambda bi, hi, qi: (hi, 0, 0, 0)),
            pl.BlockSpec((1, V_HEAD_DIM), lambda bi, hi, qi: (0, 0)),
        ],
        out_specs=pl.BlockSpec((None, t, 128 * hps), lambda bi, hi, qi: (bi, qi, hi)),
        out_shape=jax.ShapeDtypeStruct((b, s, D_ATTN), BF16),
        scratch_shapes=[pltpu.VMEM((hps, 2 * t, 2 * V_HEAD_DIM), F32),
                        pltpu.VMEM((hps, 2, 2 * t, V_HEAD_DIM), F32),
                        pltpu.VMEM((hps, 2, 2 * t, tk), F32)],
        compiler_params=_params(("parallel", "parallel", "arbitrary")),
        name="diff_attention",
    )(lam_vecs, h3, h3, h3, h3, bias_tiles, sub_norm)


def _attention_bias_tiles(rel_bias, s):
    t, tk = ATTN_Q_TILE, ATTN_K_TILE
    nh = rel_bias.shape[1]
    assert _BUCKET_START[-1] < FAR_DISTANCE <= s
    return pl.pallas_call(
        _bias_tiles_kernel,
        grid=(nh,),
        in_specs=[pl.BlockSpec(memory_space=pltpu.SMEM)],
        out_specs=pl.BlockSpec((None, N_BIAS_TILES, t, tk), lambda h: (h, 0, 0, 0)),
        out_shape=jax.ShapeDtypeStruct((nh, N_BIAS_TILES, t, tk), F32),
        compiler_params=_params(("parallel",)),
        name="attention_bias_tiles",
    )(rel_bias)


def _bucket_starts():
    max_exact = N_BUCKETS // 2
    n = np.arange(max_exact, 4 * MAX_DISTANCE, dtype=np.float64)
    x = np.log(n / max_exact) / math.log(MAX_DISTANCE / max_exact) * (N_BUCKETS - max_exact)
    frac = np.abs(x - np.round(x))[1:]
    assert frac[n[1:] < MAX_DISTANCE].min() > 1e-3
    bucket = np.minimum(max_exact + np.floor(x + 1e-9).astype(np.int64), N_BUCKETS - 1)
    starts = list(range(max_exact)) + [int(n[np.argmax(bucket >= b)]) for b in range(max_exact, N_BUCKETS)]
    assert all(b > a for a, b in zip(starts, starts[1:]))
    return tuple(starts)


_BUCKET_START = _bucket_starts()
BIAS_ROWS = 64


def _bias_tiles_kernel(rel_ref, o_ref):
    t, tk = ATTN_Q_TILE, ATTN_K_TILE
    h = pl.program_id(0)
    far_value = rel_ref[N_BUCKETS - 1, h]
    values = [(rel_ref[b, h] - far_value) * LOG2_E for b in range(N_BUCKETS - 1)]
    row = lax.broadcasted_iota(jnp.int32, (BIAS_ROWS, tk), 0)
    col = lax.broadcasted_iota(jnp.int32, (BIAS_ROWS, tk), 1)
    for idx in range(N_BIAS_TILES):
        for r in range(0, t, BIAS_ROWS):
            dist = row - col + (r + tk * idx - (t - tk))
            bias = jnp.zeros((BIAS_ROWS, tk), F32)
            for b in range(N_BUCKETS - 2, -1, -1):
                bias = jnp.where(dist < _BUCKET_START[b + 1], values[b], bias)
            o_ref[idx, r:r + BIAS_ROWS, :] = jnp.where(dist < 0, MASK_VALUE, bias)


N_STATE_TILES = N_CPLX // 128
TILES_PER_OCTET = OCTET_STATES // 128
SCAN_PITCH = 40
SUBLANES = 8
GLU_CHUNK = 256


def _s5_kernel(u_ref, bre_ref, bim_ref, cre_ref, cim_ref, lam_ref, d_ref, wglu_ref,
               o_ref, xr_ref, xi_ref, carry_ref, y_ref):
    tt = u_ref.shape[0]
    ti = pl.program_id(1)
    cur = ti % 2
    prev = 1 - cur

    @pl.when(ti == 0)
    def _():
        carry_ref[...] = jnp.zeros_like(carry_ref)
        y_ref[...] = jnp.zeros(y_ref.shape, BF16)

    def time_rows(t0, tile):
        return pl.ds(t0 * SCAN_PITCH + tile, SUBLANES, stride=SCAN_PITCH)

    u = u_ref[...]
    for j in range(N_OCTETS):
        uj = u[:, j * 128:(j + 1) * 128]
        for x_ref, b_ref in ((xr_ref, bre_ref), (xi_ref, bim_ref)):
            for n0 in range(0, TILES_PER_OCTET, 2):
                bu = jnp.dot(uj, b_ref[j, :, n0 * 128:(n0 + 2) * 128],
                             preferred_element_type=F32)
                for t0 in range(0, tt, SUBLANES):
                    for c in range(2):
                        x_ref[time_rows(t0, TILES_PER_OCTET * j + n0 + c), :] = (
                            bu[t0:t0 + SUBLANES, c * 128:(c + 1) * 128])

    lam_r = lam_ref[0]
    lam_i = lam_ref[1]
    state = [carry_ref[0], carry_ref[1]]

    def scan_steps(first, count):
        for t in range(first, first + count):
            rows = slice(t * SCAN_PITCH, t * SCAN_PITCH + N_STATE_TILES)
            sr, si = state
            nr = lam_r * sr - lam_i * si + xr_ref[rows, :]
            ni = lam_r * si + lam_i * sr + xi_ref[rows, :]
            xr_ref[rows, :] = nr
            xi_ref[rows, :] = ni
            state[0], state[1] = nr, ni

    n_gate_chunks = D_SSM // GLU_CHUNK
    n_k = D_SSM // GLU_CHUNK
    steps_per_gap = tt // (2 * n_gate_chunks * n_k)
    t_done = 0
    for c in range(n_gate_chunks):
        z = []
        for part in range(2):
            cols = slice(part * D_SSM + c * GLU_CHUNK, part * D_SSM + (c + 1) * GLU_CHUNK)
            acc = None
            for ks in range(n_k):
                kk = slice(ks * GLU_CHUNK, (ks + 1) * GLU_CHUNK)
                prod = jnp.dot(y_ref[prev, :, kk], wglu_ref[kk, cols], preferred_element_type=F32)
                acc = prod if acc is None else acc + prod
                scan_steps(t_done, steps_per_gap)
                t_done += steps_per_gap
            z.append(acc)
        o_ref[:, c * GLU_CHUNK:(c + 1) * GLU_CHUNK] = (z[0] * jax.nn.sigmoid(z[1])).astype(o_ref.dtype)
    assert t_done == tt
    carry_ref[0], carry_ref[1] = state

    def natural(x_ref, tile0):
        return jnp.concatenate(
            [jnp.concatenate([x_ref[time_rows(t0, tile0 + c), :] for c in range(2)], axis=1)
             for t0 in range(0, tt, SUBLANES)], axis=0)

    ys = []
    for j in range(N_OCTETS):
        y = None
        for x_ref, c_ref in ((xr_ref, cre_ref), (xi_ref, cim_ref)):
            for n0 in range(0, TILES_PER_OCTET, 2):
                part = jnp.dot(natural(x_ref, TILES_PER_OCTET * j + n0).astype(BF16),
                               c_ref[j, n0 * 128:(n0 + 2) * 128, :], preferred_element_type=F32)
                y = part if y is None else y + part
        ys.append(y)
    y = jnp.concatenate(ys, axis=1) + d_ref[...] * u.astype(F32)
    y_ref[cur] = jax.nn.gelu(y).astype(BF16)


def _s5_mixer(h3, bre, bim, cre, cim, lam, d_row, w_glu, tt=256):
    b, s, _ = h3.shape
    n = s // tt
    const3 = lambda bi, ti: (0, 0, 0)
    const2 = lambda bi, ti: (0, 0)
    return pl.pallas_call(
        _s5_kernel,
        grid=(b, n + 1),
        in_specs=[
            pl.BlockSpec((None, tt, D_SSM), lambda bi, ti: (bi, jnp.minimum(ti, n - 1), 3)),
            pl.BlockSpec(bre.shape, const3),
            pl.BlockSpec(bim.shape, const3),
            pl.BlockSpec(cre.shape, const3),
            pl.BlockSpec(cim.shape, const3),
            pl.BlockSpec(lam.shape, const3),
            pl.BlockSpec(d_row.shape, const2),
            pl.BlockSpec(w_glu.shape, const2),
        ],
        out_specs=pl.BlockSpec((None, tt, D_SSM), lambda bi, ti: (bi, jnp.maximum(ti - 1, 0), 0)),
        out_shape=jax.ShapeDtypeStruct((b, s, D_SSM), BF16),
        scratch_shapes=[pltpu.VMEM((tt * SCAN_PITCH, 128), F32),
                        pltpu.VMEM((tt * SCAN_PITCH, 128), F32),
                        pltpu.VMEM((2, N_STATE_TILES, 128), F32),
                        pltpu.VMEM((2, tt, D_SSM), BF16)],
        compiler_params=_params(("parallel", "arbitrary")),
        name="s5_mixer",
    )(h3, bre, bim, cre, cim, lam, d_row, w_glu)


def _s5_parameters(a_re, a_im, log_dt, b_re, b_im, c_re, c_im):
    g, p, hc = N_SSM_GROUPS, SSM_STATE, SSM_GROUP
    dt = jnp.exp(log_dt.astype(F32))[:, None]
    ar = a_re.astype(F32)
    ai = a_im.astype(F32)
    mag = jnp.exp(ar * dt)
    lb_re = mag * jnp.cos(ai * dt)
    lb_im = mag * jnp.sin(ai * dt)
    den = ar * ar + ai * ai
    nr, ni = lb_re - 1.0, lb_im
    f_re = (nr * ar + ni * ai) / den
    f_im = (ni * ar - nr * ai) / den
    br = b_re.astype(F32)
    bi = b_im.astype(F32)
    bb_re = f_re[..., None] * br - f_im[..., None] * bi
    bb_im = f_re[..., None] * bi + f_im[..., None] * br

    eye = jnp.eye(GROUPS_PER_OCTET, dtype=F32)

    def b_layout(bb):
        bb4 = bb.reshape(N_OCTETS, GROUPS_PER_OCTET, p, hc)
        return jnp.einsum('jgph,kg->jkhgp', bb4, eye).reshape(
            N_OCTETS, GROUPS_PER_OCTET * hc, OCTET_STATES).astype(BF16)

    def c_layout(cc):
        c4 = cc.astype(F32).reshape(N_OCTETS, GROUPS_PER_OCTET, hc, p)
        return jnp.einsum('jghp,kg->jgpkh', c4, eye).reshape(
            N_OCTETS, OCTET_STATES, GROUPS_PER_OCTET * hc).astype(BF16)

    lam = jnp.stack([lb_re, lb_im]).reshape(2, N_STATE_TILES, 128)
    return b_layout(bb_re), b_layout(bb_im), c_layout(c_re), -c_layout(c_im), lam


def _out_proj_kernel(a_ref, s_ref, w_ref, x_ref, gpost_ref, gpre_ref, x1_ref, xn_ref):
    mix = (jnp.dot(a_ref[...], w_ref[:D_ATTN, :], preferred_element_type=F32)
           + jnp.dot(s_ref[...], w_ref[D_ATTN:, :], preferred_element_type=F32))
    y = mix * lax.rsqrt(jnp.mean(mix * mix, axis=-1, keepdims=True) + EPS)
    x1 = x_ref[...] + y * gpost_ref[...]
    x1_ref[...] = x1
    z = x1 * lax.rsqrt(jnp.mean(x1 * x1, axis=-1, keepdims=True) + EPS)
    xn_ref[...] = (z * gpre_ref[...]).astype(xn_ref.dtype)


def _out_proj(attn2, ssm2, w_out, x2, g_post, g_pre, tm=256):
    m, d = x2.shape
    return pl.pallas_call(
        _out_proj_kernel,
        grid=(m // tm,),
        in_specs=[
            pl.BlockSpec((tm, D_ATTN), lambda i: (i, 0)),
            pl.BlockSpec((tm, D_SSM), lambda i: (i, 0)),
            pl.BlockSpec(w_out.shape, lambda i: (0, 0)),
            pl.BlockSpec((tm, d), lambda i: (i, 0)),
            pl.BlockSpec((1, d), lambda i: (0, 0)),
            pl.BlockSpec((1, d), lambda i: (0, 0)),
        ],
        out_specs=[pl.BlockSpec((tm, d), lambda i: (i, 0)),
                   pl.BlockSpec((tm, d), lambda i: (i, 0))],
        out_shape=[jax.ShapeDtypeStruct((m, d), F32),
                   jax.ShapeDtypeStruct((m, d), BF16)],
        compiler_params=_params(("parallel",)),
        name="out_proj_norm",
    )(attn2, ssm2, w_out, x2, g_post, g_pre)


CONV_HALO = 8
CONV_CHUNK = 256
CONV_ROWS = 64


def _up_conv_kernel(x_ref, wg_ref, wv_ref, cpg_ref, cpv_ref, o_ref,
                    h_ref, h2_ref, wgb_ref, wvb_ref, *, tiles_per_seq):
    tm = x_ref.shape[0]
    tn = o_ref.shape[1]
    n_chunks = tn // CONV_CHUNK
    i = pl.program_id(1)

    @pl.when(i == 0)
    def _():
        h_ref[...] = jnp.zeros(h_ref.shape, F32)
        h2_ref[...] = jnp.zeros(h2_ref.shape, F32)
        wgb_ref[...] = wg_ref[...].astype(BF16)
        wvb_ref[...] = wv_ref[...].astype(BF16)

    seq_start = (i % tiles_per_seq) == 0

    def body(cur_ref, prev_ref):
        x = x_ref[...]

        def conv_gate(c, r):
            cols = slice(c * CONV_CHUNK, (c + 1) * CONV_CHUNK)

            def conv(k, cp_ref):
                rows = lambda shift: slice(CONV_HALO - shift + r, CONV_HALO - shift + r + CONV_ROWS)
                par = lambda n: jnp.tile(cp_ref[n, :, cols], (CONV_ROWS // SUBLANES, 1))
                return (par(0) * prev_ref[k, rows(2), :] + par(1) * prev_ref[k, rows(1), :]
                        + par(2) * prev_ref[k, rows(0), :] + par(3))

            g = conv(2 * c, cpg_ref)
            v = conv(2 * c + 1, cpv_ref)
            o_ref[r:r + CONV_ROWS, cols] = (jax.nn.silu(g) * v).astype(o_ref.dtype)

        n_slices = tm // CONV_ROWS
        k_slice = x.shape[1] // n_slices
        for c in range(n_chunks):
            cols = slice(c * CONV_CHUNK, (c + 1) * CONV_CHUNK)
            acc_g = acc_v = None
            for s in range(n_slices):
                ks = slice(s * k_slice, (s + 1) * k_slice)
                pg = jnp.dot(x[:, ks], wgb_ref[ks, cols], preferred_element_type=F32)
                pv = jnp.dot(x[:, ks], wvb_ref[ks, cols], preferred_element_type=F32)
                acc_g = pg if acc_g is None else acc_g + pg
                acc_v = pv if acc_v is None else acc_v + pv
                conv_gate(c, s * CONV_ROWS)
            cur_ref[2 * c, CONV_HALO:, :] = acc_g
            cur_ref[2 * c + 1, CONV_HALO:, :] = acc_v
        for k in range(2 * n_chunks):
            tail = prev_ref[k, tm:tm + CONV_HALO, :]
            cur_ref[k, 0:CONV_HALO, :] = jnp.where(seq_start, 0.0, tail)

    @pl.when(i % 2 == 0)
    def _():
        body(h_ref, h2_ref)

    @pl.when(i % 2 == 1)
    def _():
        body(h2_ref, h_ref)


def _up_conv(xn2, w_up, conv_w, conv_b, seq_len, tm=512, tn=512):
    m, d = xn2.shape
    conv_params = jnp.broadcast_to(jnp.concatenate([conv_w, conv_b], axis=0)[:, None, :],
                                   (4, SUBLANES, conv_w.shape[1]))
    nj = D_FF // tn
    ni = m // tm
    kern = functools.partial(_up_conv_kernel, tiles_per_seq=seq_len // tm)
    return pl.pallas_call(
        kern,
        grid=(nj, ni + 1),
        in_specs=[
            pl.BlockSpec((tm, d), lambda j, i: (jnp.minimum(i, ni - 1), 0)),
            pl.BlockSpec((d, tn), lambda j, i: (0, j)),
            pl.BlockSpec((d, tn), lambda j, i: (0, nj + j)),
            pl.BlockSpec((4, SUBLANES, tn), lambda j, i: (0, 0, j)),
            pl.BlockSpec((4, SUBLANES, tn), lambda j, i: (0, 0, nj + j)),
        ],
        out_specs=pl.BlockSpec((tm, tn), lambda j, i: (jnp.maximum(i - 1, 0), j)),
        out_shape=jax.ShapeDtypeStruct((m, D_FF), BF16),
        scratch_shapes=[pltpu.VMEM((2 * (tn // CONV_CHUNK), tm + CONV_HALO, CONV_CHUNK), F32),
                        pltpu.VMEM((2 * (tn // CONV_CHUNK), tm + CONV_HALO, CONV_CHUNK), F32),
                        pltpu.VMEM((d, tn), BF16),
                        pltpu.VMEM((d, tn), BF16)],
        compiler_params=_params(("parallel", "arbitrary")),
        name="up_conv_gate",
    )(xn2, w_up, w_up, conv_params, conv_params)


def _down_proj_kernel(a_ref, w_ref, x_ref, g_ref, o_ref, acc_ref):
    k = pl.program_id(1)

    @pl.when(k == 0)
    def _():
        acc_ref[...] = jnp.zeros_like(acc_ref)

    acc_ref[...] += jnp.dot(a_ref[...], w_ref[...], preferred_element_type=F32)

    @pl.when(k == pl.num_programs(1) - 1)
    def _():
        f = acc_ref[...]
        y = f * lax.rsqrt(jnp.mean(f * f, axis=-1, keepdims=True) + EPS)
        o_ref[...] = x_ref[...] + y * g_ref[...]


def _down_proj(gated, w_down, x1, g_post, tm=1024, tk=512):
    m, kdim = gated.shape
    d = w_down.shape[1]
    return pl.pallas_call(
        _down_proj_kernel,
        grid=(m // tm, kdim // tk),
        in_specs=[
            pl.BlockSpec((tm, tk), lambda i, k: (i, k)),
            pl.BlockSpec((tk, d), lambda i, k: (k, 0)),
            pl.BlockSpec((tm, d), lambda i, k: (i, 0)),
            pl.BlockSpec((1, d), lambda i, k: (0, 0)),
        ],
        out_specs=pl.BlockSpec((tm, d), lambda i, k: (i, 0)),
        out_shape=jax.ShapeDtypeStruct((m, d), F32),
        scratch_shapes=[pltpu.VMEM((tm, d), F32)],
        compiler_params=_params(("parallel", "arbitrary")),
        name="down_proj_norm",
    )(gated, w_down, x1, g_post)


def kernel(x, ln_pre_mix, ln_post_mix, ln_pre_ffn, ln_post_ffn, w_in, lam_q1, lam_k1, lam_q2, lam_k2, attn_sub_norm, rel_bias, ssm_a_re, ssm_a_im, ssm_log_dt, ssm_b_re, ssm_b_im, ssm_c_re, ssm_c_im, ssm_d, ssm_w_glu, w_out, w_up, conv_w, conv_b, w_down):
    b, s, d = x.shape
    assert d == D_MODEL and s % ATTN_Q_TILE == 0 and ATTN_Q_TILE % ATTN_K_TILE == 0
    l = 0
    x2 = x.reshape(b * s, d)

    col_scale = jnp.where(jnp.arange(D_IN) < D_QK, QK_SCALE, 1.0).astype(F32)
    h = _norm_proj(x2, ln_pre_mix[l][None, :], (w_in[l] * col_scale[None, :]).astype(BF16))
    h3 = h.reshape(b, s, D_IN)

    lam_vecs = jnp.stack([lam_q1[l], lam_k1[l], lam_q2[l], lam_k2[l]]).astype(F32)
    bias_tiles = _attention_bias_tiles(rel_bias.astype(F32), s)
    attn = _diff_attention(h3, lam_vecs, bias_tiles, attn_sub_norm[l][None, :].astype(F32))

    bre, bim, cre, cim, coef = _s5_parameters(ssm_a_re[l], ssm_a_im[l], ssm_log_dt[l],
                                              ssm_b_re[l], ssm_b_im[l], ssm_c_re[l], ssm_c_im[l])
    ssm = _s5_mixer(h3, bre, bim, cre, cim, coef,
                    ssm_d[l].reshape(1, D_SSM).astype(F32), ssm_w_glu[l].astype(BF16))

    x1, xn2 = _out_proj(attn.reshape(b * s, D_ATTN), ssm.reshape(b * s, D_SSM),
                        w_out[l].astype(BF16), x2, ln_post_mix[l][None, :], ln_pre_ffn[l][None, :])

    gated = _up_conv(xn2, w_up[l].astype(F32), conv_w[l].reshape(3, 2 * D_FF),
                     conv_b[l].reshape(1, 2 * D_FF), s)
    y = _down_proj(gated, w_down[l].astype(BF16), x1, ln_post_ffn[l][None, :])
    return y.reshape(b, s, d)
```

```python
import functools
import math

import numpy as np
import jax
import jax.numpy as jnp
from jax import lax
from jax.experimental import pallas as pl
from jax.experimental.pallas import tpu as pltpu

D_MODEL = 2048
D_ATTN = 1024
D_SSM = 1024
QK_HEAD_DIM = 64
V_HEAD_DIM = 128
N_ATTN_HEADS = 8
D_QK = 1024
SSM_GROUP = 16
N_SSM_GROUPS = 64
SSM_STATE = 64
D_IN = 4096
D_FF = 5632
N_BUCKETS = 32
MAX_DISTANCE = 128
EPS = 1e-6
LAMBDA_INIT = 0.8 - 0.6 * math.exp(-0.3 * 0)
LOG2_E = math.log2(math.e)
QK_SCALE = QK_HEAD_DIM ** -0.5 * LOG2_E

N_CPLX = N_SSM_GROUPS * SSM_STATE
GROUPS_PER_OCTET = 8
N_OCTETS = N_SSM_GROUPS // GROUPS_PER_OCTET
OCTET_STATES = GROUPS_PER_OCTET * SSM_STATE

ATTN_Q_TILE = 512
ATTN_K_TILE = 256
ATTN_HEADS_PER_STEP = 1
FAR_DISTANCE = 2 * MAX_DISTANCE + 1
N_BIAS_TILES = -(-(FAR_DISTANCE + ATTN_Q_TILE - 1) // ATTN_K_TILE)
MASK_VALUE = -1e30

VMEM_LIMIT = 56 * 1024 * 1024

F32 = jnp.float32
BF16 = jnp.bfloat16


def _params(sem, flags=None):
    return pltpu.CompilerParams(dimension_semantics=sem, vmem_limit_bytes=VMEM_LIMIT, flags=flags)


def _norm_proj_kernel(x_ref, g_ref, w_ref, o_ref, xn_ref):
    @pl.when(pl.program_id(1) == 0)
    def _():
        x = x_ref[...]
        y = x * lax.rsqrt(jnp.mean(x * x, axis=-1, keepdims=True) + EPS)
        xn_ref[...] = (y * g_ref[...]).astype(BF16)

    o_ref[...] = jnp.dot(xn_ref[...], w_ref[...], preferred_element_type=F32).astype(o_ref.dtype)


def _norm_proj(x2, g, w, tm=1024, tn=1024):
    m, d = x2.shape
    n = w.shape[1]
    return pl.pallas_call(
        _norm_proj_kernel,
        grid=(m // tm, n // tn),
        in_specs=[
            pl.BlockSpec((tm, d), lambda i, j: (i, 0)),
            pl.BlockSpec((1, d), lambda i, j: (0, 0)),
            pl.BlockSpec((d, tn), lambda i, j: (0, j)),
        ],
        out_specs=pl.BlockSpec((tm, tn), lambda i, j: (i, j)),
        out_shape=jax.ShapeDtypeStruct((m, n), BF16),
        scratch_shapes=[pltpu.VMEM((tm, d), BF16)],
        compiler_params=_params(("parallel", "arbitrary")),
        name="norm_in_proj",
    )(x2, g, w)


def _attn_kernel(lam_ref, q_ref, qn_ref, k_ref, v_ref, bias_ref, sub_ref, o_ref,
                 acc_ref, m_ref, s_ref):
    t = ATTN_Q_TILE
    tk = ATTN_K_TILE
    qi = pl.program_id(2)
    last = (t // tk) * (qi + 1) - 1

    def stack_maps(q):
        lane = lax.broadcasted_iota(jnp.int32, q.shape, 1)
        zero = jnp.zeros_like(q)
        return jnp.concatenate([jnp.where(lane < QK_HEAD_DIM, q, zero),
                                jnp.where(lane >= QK_HEAD_DIM, q, zero)], axis=0)

    heads = range(ATTN_HEADS_PER_STEP)
    head_lanes = lambda hd: slice(hd * 128, (hd + 1) * 128)
    q2 = [stack_maps(q_ref[:, head_lanes(hd)]) for hd in heads]

    ones = jnp.ones((tk, V_HEAD_DIM), BF16)

    def scores(kt, hd):
        k = k_ref[pl.ds(pl.multiple_of(kt * tk, tk), tk), head_lanes(hd)]
        return lax.dot_general(q2[hd], k, (((1,), (1,)), ((), ())), preferred_element_type=F32)

    def v_ext(kt, hd):
        v = v_ref[pl.ds(pl.multiple_of(kt * tk, tk), tk), head_lanes(hd)]
        return jnp.concatenate([v, ones], axis=1)

    def row_max(s):
        m = s[:, :128]
        for c in range(1, tk // 128):
            m = jnp.maximum(m, s[:, c * 128:(c + 1) * 128])
        return jnp.broadcast_to(jnp.max(m, axis=-1, keepdims=True), (2 * t, 128))

    def store_scores(kt, slot, bias_idx=None):
        for hd in heads:
            s = scores(kt, hd)
            if bias_idx is not None:
                bias = bias_ref[hd, bias_idx]
                s = s + jnp.concatenate([bias, bias], axis=0)
            s_ref[hd, slot] = s
            m_ref[hd, slot] = jnp.maximum(m_ref[hd, 1 - slot], row_max(s))

    def fused(kt, hd, bias_idx=None):
        s = scores(kt, hd)
        if bias_idx is not None:
            bias = bias_ref[hd, bias_idx]
            s = s + jnp.concatenate([bias, bias], axis=0)
        m_old = m_ref[hd, 0]
        m_new = jnp.maximum(m_old, row_max(s))
        m_ref[hd, 0] = m_new
        a = jnp.exp2(m_old - m_new)
        p = jnp.exp2(s - jnp.concatenate([m_new] * (tk // 128), axis=1)).astype(BF16)
        return (acc_ref[hd] * jnp.concatenate([a, a], axis=1)
                + jnp.dot(p, v_ext(kt, hd), preferred_element_type=F32))

    def step(j, bias_idx=None):
        for hd in heads:
            acc_ref[hd] = fused(j, hd, bias_idx)

    def arm():
        for hd in heads:
            m_ref[hd, 0] = jnp.full(m_ref.shape[2:], MASK_VALUE, F32)
            acc_ref[hd] = jnp.zeros(acc_ref.shape[1:], F32)

    lam_v = lam_ref[...]
    lam = (jnp.exp(jnp.sum(lam_v[0:1] * lam_v[1:2], axis=-1, keepdims=True))
           - jnp.exp(jnp.sum(lam_v[2:3] * lam_v[3:4], axis=-1, keepdims=True))
           + LAMBDA_INIT)

    assert N_BIAS_TILES == 3 and t == 2 * tk

    arm()

    n_main = jnp.maximum(last - 2, 0)

    def steps(first, count):
        for k in range(count):
            step(first + k)

    def oct_body(i, carry):
        steps(8 * i, 8)
        return carry

    lax.fori_loop(0, n_main // 8, oct_body, 0)
    done = 8 * (n_main // 8)

    @pl.when(n_main % 8 >= 4)
    def _():
        steps(done, 4)

    @pl.when(n_main % 4 >= 2)
    def _():
        steps(done + 4 * ((n_main % 8) // 4), 2)

    @pl.when(n_main % 2 == 1)
    def _():
        steps(n_main - 1, 1)

    def drain():
        step(last - 1, bias_idx=1)
        for hd in heads:
            acc = fused(last, hd, bias_idx=0)
            out_a = acc[:t, :V_HEAD_DIM] / acc[:t, V_HEAD_DIM:]
            out_b = acc[t:, :V_HEAD_DIM] / acc[t:, V_HEAD_DIM:]
            o = out_a - lam * out_b
            y = o * lax.rsqrt(jnp.mean(o * o, axis=-1, keepdims=True) + EPS)
            o_ref[:, head_lanes(hd)] = ((y * sub_ref[...]) * (1.0 - LAMBDA_INIT)).astype(o_ref.dtype)

    @pl.when(qi > 0)
    def _():
        step(last - 2, bias_idx=2)
        drain()

    @pl.when(qi == 0)
    def _():
        drain()


def _diff_attention(h3, lam_vecs, bias_tiles, sub_norm):
    b, s, _ = h3.shape
    t = ATTN_Q_TILE
    tk = ATTN_K_TILE
    hps = ATTN_HEADS_PER_STEP
    ng = N_ATTN_HEADS // hps
    return pl.pallas_call(
        _attn_kernel,
        grid=(b, ng, s // t),
        in_specs=[
            pl.BlockSpec((4, QK_HEAD_DIM), lambda bi, hi, qi: (0, 0)),
            pl.BlockSpec((None, t, 128 * hps), lambda bi, hi, qi: (bi, qi, hi)),
            pl.BlockSpec((None, t, 128 * hps),
                         lambda bi, hi, qi: (bi, jnp.minimum(qi + 1, s // t - 1), hi)),
            pl.BlockSpec((None, s, 128 * hps), lambda bi, hi, qi: (bi, 0, ng + hi)),
            pl.BlockSpec((None, s, 128 * hps), lambda bi, hi, qi: (bi, 0, 2 * ng + hi)),
            pl.BlockSpec((hps, N_BIAS_TILES, t, tk), lambda bi, hi, qi: (hi, 0, 0, 0)),
            pl.BlockSpec((1, V_HEAD_DIM), lambda bi, hi, qi: (0, 0)),
        ],
        out_specs=pl.BlockSpec((None, t, 128 * hps), lambda bi, hi, qi: (bi, qi, hi)),
        out_shape=jax.ShapeDtypeStruct((b, s, D_ATTN), BF16),
        scratch_shapes=[pltpu.VMEM((hps, 2 * t, 2 * V_HEAD_DIM), F32),
                        pltpu.VMEM((hps, 2, 2 * t, V_HEAD_DIM), F32),
                        pltpu.VMEM((hps, 2, 2 * t, tk), F32)],
        compiler_params=_params(("parallel", "parallel", "arbitrary")),
        name="diff_attention",
    )(lam_vecs, h3, h3, h3, h3, bias_tiles, sub_norm)


def _attention_bias_tiles(rel_bias, s):
    t, tk = ATTN_Q_TILE, ATTN_K_TILE
    nh = rel_bias.shape[1]
    assert _BUCKET_START[-1] < FAR_DISTANCE <= s
    return pl.pallas_call(
        _bias_tiles_kernel,
        grid=(nh,),
        in_specs=[pl.BlockSpec(memory_space=pltpu.SMEM)],
        out_specs=pl.BlockSpec((None, N_BIAS_TILES, t, tk), lambda h: (h, 0, 0, 0)),
        out_shape=jax.ShapeDtypeStruct((nh, N_BIAS_TILES, t, tk), F32),
        compiler_params=_params(("parallel",)),
        name="attention_bias_tiles",
    )(rel_bias)


def _bucket_starts():
    max_exact = N_BUCKETS // 2
    n = np.arange(max_exact, 4 * MAX_DISTANCE, dtype=np.float64)
    x = np.log(n / max_exact) / math.log(MAX_DISTANCE / max_exact) * (N_BUCKETS - max_exact)
    frac = np.abs(x - np.round(x))[1:]
    assert frac[n[1:] < MAX_DISTANCE].min() > 1e-3
    bucket = np.minimum(max_exact + np.floor(x + 1e-9).astype(np.int64), N_BUCKETS - 1)
    starts = list(range(max_exact)) + [int(n[np.argmax(bucket >= b)]) for b in range(max_exact, N_BUCKETS)]
    assert all(b > a for a, b in zip(starts, starts[1:]))
    return tuple(starts)


_BUCKET_START = _bucket_starts()
BIAS_ROWS = 64


def _bias_tiles_kernel(rel_ref, o_ref):
    t, tk = ATTN_Q_TILE, ATTN_K_TILE
    h = pl.program_id(0)
    far_value = rel_ref[N_BUCKETS - 1, h]
    values = [(rel_ref[b, h] - far_value) * LOG2_E for b in range(N_BUCKETS - 1)]
    row = lax.broadcasted_iota(jnp.int32, (BIAS_ROWS, tk), 0)
    col = lax.broadcasted_iota(jnp.int32, (BIAS_ROWS, tk), 1)
    for idx in range(N_BIAS_TILES):
        for r in range(0, t, BIAS_ROWS):
            dist = row - col + (r + tk * idx - (t - tk))
            bias = jnp.zeros((BIAS_ROWS, tk), F32)
            for b in range(N_BUCKETS - 2, -1, -1):
                bias = jnp.where(dist < _BUCKET_START[b + 1], values[b], bias)
            o_ref[idx, r:r + BIAS_ROWS, :] = jnp.where(dist < 0, MASK_VALUE, bias)


N_STATE_TILES = N_CPLX // 128
TILES_PER_OCTET = OCTET_STATES // 128
SCAN_PITCH = 40
SUBLANES = 8
GLU_CHUNK = 256


def _s5_kernel(u_ref, bre_ref, bim_ref, cre_ref, cim_ref, lam_ref, d_ref, wglu_ref,
               o_ref, xr_ref, xi_ref, carry_ref, y_ref):
    tt = u_ref.shape[0]
    ti = pl.program_id(1)
    cur = ti % 2
    prev = 1 - cur

    @pl.when(ti == 0)
    def _():
        carry_ref[...] = jnp.zeros_like(carry_ref)
        y_ref[...] = jnp.zeros(y_ref.shape, BF16)

    def time_rows(t0, tile):
        return pl.ds(t0 * SCAN_PITCH + tile, SUBLANES, stride=SCAN_PITCH)

    u = u_ref[...]
    for j in range(N_OCTETS):
        uj = u[:, j * 128:(j + 1) * 128]
        for x_ref, b_ref in ((xr_ref, bre_ref), (xi_ref, bim_ref)):
            for n0 in range(0, TILES_PER_OCTET, 2):
                bu = jnp.dot(uj, b_ref[j, :, n0 * 128:(n0 + 2) * 128],
                             preferred_element_type=F32)
                for t0 in range(0, tt, SUBLANES):
                    for c in range(2):
                        x_ref[time_rows(t0, TILES_PER_OCTET * j + n0 + c), :] = (
                            bu[t0:t0 + SUBLANES, c * 128:(c + 1) * 128])

    lam_r = lam_ref[0]
    lam_i = lam_ref[1]
    state = [carry_ref[0], carry_ref[1]]

    def scan_steps(first, count):
        for t in range(first, first + count):
            rows = slice(t * SCAN_PITCH, t * SCAN_PITCH + N_STATE_TILES)
            sr, si = state
            nr = lam_r * sr - lam_i * si + xr_ref[rows, :]
            ni = lam_r * si + lam_i * sr + xi_ref[rows, :]
            xr_ref[rows, :] = nr
            xi_ref[rows, :] = ni
            state[0], state[1] = nr, ni

    n_gate_chunks = D_SSM // GLU_CHUNK
    n_k = D_SSM // GLU_CHUNK
    steps_per_gap = tt // (2 * n_gate_chunks * n_k)
    t_done = 0
    for c in range(n_gate_chunks):
        z = []
        for part in range(2):
            cols = slice(part * D_SSM + c * GLU_CHUNK, part * D_SSM + (c + 1) * GLU_CHUNK)
            acc = None
            for ks in range(n_k):
                kk = slice(ks * GLU_CHUNK, (ks + 1) * GLU_CHUNK)
                prod = jnp.dot(y_ref[prev, :, kk], wglu_ref[kk, cols], preferred_element_type=F32)
                acc = prod if acc is None else acc + prod
                scan_steps(t_done, steps_per_gap)
                t_done += steps_per_gap
            z.append(acc)
        o_ref[:, c * GLU_CHUNK:(c + 1) * GLU_CHUNK] = (z[0] * jax.nn.sigmoid(z[1])).astype(o_ref.dtype)
    assert t_done == tt
    carry_ref[0], carry_ref[1] = state

    def natural(x_ref, tile0):
        return jnp.concatenate(
            [jnp.concatenate([x_ref[time_rows(t0, tile0 + c), :] for c in range(2)], axis=1)
             for t0 in range(0, tt, SUBLANES)], axis=0)

    ys = []
    for j in range(N_OCTETS):
        y = None
        for x_ref, c_ref in ((xr_ref, cre_ref), (xi_ref, cim_ref)):
            for n0 in range(0, TILES_PER_OCTET, 2):
                part = jnp.dot(natural(x_ref, TILES_PER_OCTET * j + n0).astype(BF16),
                               c_ref[j, n0 * 128:(n0 + 2) * 128, :], preferred_element_type=F32)
                y = part if y is None else y + part
        ys.append(y)
    y = jnp.concatenate(ys, axis=1) + d_ref[...] * u.astype(F32)
    y_ref[cur] = jax.nn.gelu(y).astype(BF16)


def _s5_mixer(h3, bre, bim, cre, cim, lam, d_row, w_glu, tt=256):
    b, s, _ = h3.shape
    n = s // tt
    const3 = lambda bi, ti: (0, 0, 0)
    const2 = lambda bi, ti: (0, 0)
    return pl.pallas_call(
        _s5_kernel,
        grid=(b, n + 1),
        in_specs=[
            pl.BlockSpec((None, tt, D_SSM), lambda bi, ti: (bi, jnp.minimum(ti, n - 1), 3)),
            pl.BlockSpec(bre.shape, const3),
            pl.BlockSpec(bim.shape, const3),
            pl.BlockSpec(cre.shape, const3),
            pl.BlockSpec(cim.shape, const3),
            pl.BlockSpec(lam.shape, const3),
            pl.BlockSpec(d_row.shape, const2),
            pl.BlockSpec(w_glu.shape, const2),
        ],
        out_specs=pl.BlockSpec((None, tt, D_SSM), lambda bi, ti: (bi, jnp.maximum(ti - 1, 0), 0)),
        out_shape=jax.ShapeDtypeStruct((b, s, D_SSM), BF16),
        scratch_shapes=[pltpu.VMEM((tt * SCAN_PITCH, 128), F32),
                        pltpu.VMEM((tt * SCAN_PITCH, 128), F32),
                        pltpu.VMEM((2, N_STATE_TILES, 128), F32),
                        pltpu.VMEM((2, tt, D_SSM), BF16)],
        compiler_params=_params(("parallel", "arbitrary")),
        name="s5_mixer",
    )(h3, bre, bim, cre, cim, lam, d_row, w_glu)


def _s5_parameters(a_re, a_im, log_dt, b_re, b_im, c_re, c_im):
    g, p, hc = N_SSM_GROUPS, SSM_STATE, SSM_GROUP
    dt = jnp.exp(log_dt.astype(F32))[:, None]
    ar = a_re.astype(F32)
    ai = a_im.astype(F32)
    mag = jnp.exp(ar * dt)
    lb_re = mag * jnp.cos(ai * dt)
    lb_im = mag * jnp.sin(ai * dt)
    den = ar * ar + ai * ai
    nr, ni = lb_re - 1.0, lb_im
    f_re = (nr * ar + ni * ai) / den
    f_im = (ni * ar - nr * ai) / den
    br = b_re.astype(F32)
    bi = b_im.astype(F32)
    bb_re = f_re[..., None] * br - f_im[..., None] * bi
    bb_im = f_re[..., None] * bi + f_im[..., None] * br

    eye = jnp.eye(GROUPS_PER_OCTET, dtype=F32)

    def b_layout(bb):
        bb4 = bb.reshape(N_OCTETS, GROUPS_PER_OCTET, p, hc)
        return jnp.einsum('jgph,kg->jkhgp', bb4, eye).reshape(
            N_OCTETS, GROUPS_PER_OCTET * hc, OCTET_STATES).astype(BF16)

    def c_layout(cc):
        c4 = cc.astype(F32).reshape(N_OCTETS, GROUPS_PER_OCTET, hc, p)
        return jnp.einsum('jghp,kg->jgpkh', c4, eye).reshape(
            N_OCTETS, OCTET_STATES, GROUPS_PER_OCTET * hc).astype(BF16)

    lam = jnp.stack([lb_re, lb_im]).reshape(2, N_STATE_TILES, 128)
    return b_layout(bb_re), b_layout(bb_im), c_layout(c_re), -c_layout(c_im), lam


def _out_proj_kernel(a_ref, s_ref, w_ref, x_ref, gpost_ref, gpre_ref, x1_ref, xn_ref):
    mix = (jnp.dot(a_ref[...], w_ref[:D_ATTN, :], preferred_element_type=F32)
           + jnp.dot(s_ref[...], w_ref[D_ATTN:, :], preferred_element_type=F32))
    y = mix * lax.rsqrt(jnp.mean(mix * mix, axis=-1, keepdims=True) + EPS)
    x1 = x_ref[...] + y * gpost_ref[...]
    x1_ref[...] = x1
    z = x1 * lax.rsqrt(jnp.mean(x1 * x1, axis=-1, keepdims=True) + EPS)
    xn_ref[...] = (z * gpre_ref[...]).astype(xn_ref.dtype)


def _out_proj(attn2, ssm2, w_out, x2, g_post, g_pre, tm=256):
    m, d = x2.shape
    return pl.pallas_call(
        _out_proj_kernel,
        grid=(m // tm,),
        in_specs=[
            pl.BlockSpec((tm, D_ATTN), lambda i: (i, 0)),
            pl.BlockSpec((tm, D_SSM), lambda i: (i, 0)),
            pl.BlockSpec(w_out.shape, lambda i: (0, 0)),
            pl.BlockSpec((tm, d), lambda i: (i, 0)),
            pl.BlockSpec((1, d), lambda i: (0, 0)),
            pl.BlockSpec((1, d), lambda i: (0, 0)),
        ],
        out_specs=[pl.BlockSpec((tm, d), lambda i: (i, 0)),
                   pl.BlockSpec((tm, d), lambda i: (i, 0))],
        out_shape=[jax.ShapeDtypeStruct((m, d), F32),
                   jax.ShapeDtypeStruct((m, d), BF16)],
        compiler_params=_params(("parallel",)),
        name="out_proj_norm",
    )(attn2, ssm2, w_out, x2, g_post, g_pre)


CONV_HALO = 8
CONV_CHUNK = 256
CONV_ROWS = 64


def _up_conv_kernel(x_ref, wg_ref, wv_ref, cpg_ref, cpv_ref, o_ref,
                    h_ref, h2_ref, wgb_ref, wvb_ref, *, tiles_per_seq):
    tm = x_ref.shape[0]
    tn = o_ref.shape[1]
    n_chunks = tn // CONV_CHUNK
    i = pl.program_id(1)

    @pl.when(i == 0)
    def _():
        h_ref[...] = jnp.zeros(h_ref.shape, F32)
        h2_ref[...] = jnp.zeros(h2_ref.shape, F32)
        wgb_ref[...] = wg_ref[...].astype(BF16)
        wvb_ref[...] = wv_ref[...].astype(BF16)

    seq_start = (i % tiles_per_seq) == 0

    def body(cur_ref, prev_ref):
        x = x_ref[...]

        def conv_gate(c, r):
            cols = slice(c * CONV_CHUNK, (c + 1) * CONV_CHUNK)

            def conv(k, cp_ref):
                rows = lambda shift: slice(CONV_HALO - shift + r, CONV_HALO - shift + r + CONV_ROWS)
                par = lambda n: jnp.tile(cp_ref[n, :, cols], (CONV_ROWS // SUBLANES, 1))
                return (par(0) * prev_ref[k, rows(2), :] + par(1) * prev_ref[k, rows(1), :]
                        + par(2) * prev_ref[k, rows(0), :] + par(3))

            g = conv(2 * c, cpg_ref)
            v = conv(2 * c + 1, cpv_ref)
            o_ref[r:r + CONV_ROWS, cols] = (jax.nn.silu(g) * v).astype(o_ref.dtype)

        n_slices = tm // CONV_ROWS
        k_slice = x.shape[1] // n_slices
        for c in range(n_chunks):
            cols = slice(c * CONV_CHUNK, (c + 1) * CONV_CHUNK)
            acc_g = acc_v = None
            for s in range(n_slices):
                ks = slice(s * k_slice, (s + 1) * k_slice)
                pg = jnp.dot(x[:, ks], wgb_ref[ks, cols], preferred_element_type=F32)
                pv = jnp.dot(x[:, ks], wvb_ref[ks, cols], preferred_element_type=F32)
                acc_g = pg if acc_g is None else acc_g + pg
                acc_v = pv if acc_v is None else acc_v + pv
                conv_gate(c, s * CONV_ROWS)
            cur_ref[2 * c, CONV_HALO:, :] = acc_g
            cur_ref[2 * c + 1, CONV_HALO:, :] = acc_v
        for k in range(2 * n_chunks):
            tail = prev_ref[k, tm:tm + CONV_HALO, :]
            cur_ref[k, 0:CONV_HALO, :] = jnp.where(seq_start, 0.0, tail)

    @pl.when(i % 2 == 0)
    def _():
        body(h_ref, h2_ref)

    @pl.when(i % 2 == 1)
    def _():
        body(h2_ref, h_ref)


def _up_conv(xn2, w_up, conv_w, conv_b, seq_len, tm=512, tn=512):
    m, d = xn2.shape
    conv_params = jnp.broadcast_to(jnp.concatenate([conv_w, conv_b], axis=0)[:, None, :],
                                   (4, SUBLANES, conv_w.shape[1]))
    nj = D_FF // tn
    ni = m // tm
    kern = functools.partial(_up_conv_kernel, tiles_per_seq=seq_len // tm)
    return pl.pallas_call(
        kern,
        grid=(nj, ni + 1),
        in_specs=[
            pl.BlockSpec((tm, d), lambda j, i: (jnp.minimum(i, ni - 1), 0)),
            pl.BlockSpec((d, tn), lambda j, i: (0, j)),
            pl.BlockSpec((d, tn), lambda j, i: (0, nj + j)),
            pl.BlockSpec((4, SUBLANES, tn), lambda j, i: (0, 0, j)),
            pl.BlockSpec((4, SUBLANES, tn), lambda j, i: (0, 0, nj + j)),
        ],
        out_specs=pl.BlockSpec((tm, tn), lambda j, i: (jnp.maximum(i - 1, 0), j)),
        out_shape=jax.ShapeDtypeStruct((m, D_FF), BF16),
        scratch_shapes=[pltpu.VMEM((2 * (tn // CONV_CHUNK), tm + CONV_HALO, CONV_CHUNK), F32),
                        pltpu.VMEM((2 * (tn // CONV_CHUNK), tm + CONV_HALO, CONV_CHUNK), F32),
                        pltpu.VMEM((d, tn), BF16),
                        pltpu.VMEM((d, tn), BF16)],
        compiler_params=_params(("parallel", "arbitrary")),
        name="up_conv_gate",
    )(xn2, w_up, w_up, conv_params, conv_params)


def _down_proj_kernel(a_ref, w_ref, x_ref, g_ref, o_ref, acc_ref):
    k = pl.program_id(1)

    @pl.when(k == 0)
    def _():
        acc_ref[...] = jnp.zeros_like(acc_ref)

    acc_ref[...] += jnp.dot(a_ref[...], w_ref[...], preferred_element_type=F32)

    @pl.when(k == pl.num_programs(1) - 1)
    def _():
        f = acc_ref[...]
        y = f * lax.rsqrt(jnp.mean(f * f, axis=-1, keepdims=True) + EPS)
        o_ref[...] = x_ref[...] + y * g_ref[...]


def _down_proj(gated, w_down, x1, g_post, tm=1024, tk=512):
    m, kdim = gated.shape
    d = w_down.shape[1]
    return pl.pallas_call(
        _down_proj_kernel,
        grid=(m // tm, kdim // tk),
        in_specs=[
            pl.BlockSpec((tm, tk), lambda i, k: (i, k)),
            pl.BlockSpec((tk, d), lambda i, k: (k, 0)),
            pl.BlockSpec((tm, d), lambda i, k: (i, 0)),
            pl.BlockSpec((1, d), lambda i, k: (0, 0)),
        ],
        out_specs=pl.BlockSpec((tm, d), lambda i, k: (i, 0)),
        out_shape=jax.ShapeDtypeStruct((m, d), F32),
        scratch_shapes=[pltpu.VMEM((tm, d), F32)],
        compiler_params=_params(("parallel", "arbitrary")),
        name="down_proj_norm",
    )(gated, w_down, x1, g_post)


def kernel(x, ln_pre_mix, ln_post_mix, ln_pre_ffn, ln_post_ffn, w_in, lam_q1, lam_k1, lam_q2, lam_k2, attn_sub_norm, rel_bias, ssm_a_re, ssm_a_im, ssm_log_dt, ssm_b_re, ssm_b_im, ssm_c_re, ssm_c_im, ssm_d, ssm_w_glu, w_out, w_up, conv_w, conv_b, w_down):
    b, s, d = x.shape
    assert d == D_MODEL and s % ATTN_Q_TILE == 0 and ATTN_Q_TILE % ATTN_K_TILE == 0
    l = 0
    x2 = x.reshape(b * s, d)

    col_scale = jnp.where(jnp.arange(D_IN) < D_QK, QK_SCALE, 1.0).astype(F32)
    h = _norm_proj(x2, ln_pre_mix[l][None, :], (w_in[l] * col_scale[None, :]).astype(BF16))
    h3 = h.reshape(b, s, D_IN)

    lam_vecs = jnp.stack([lam_q1[l], lam_k1[l], lam_q2[l], lam_k2[l]]).astype(F32)
    bias_tiles = _attention_bias_tiles(rel_bias.astype(F32), s)
    attn = _diff_attention(h3, lam_vecs, bias_tiles, attn_sub_norm[l][None, :].astype(F32))

    bre, bim, cre, cim, coef = _s5_parameters(ssm_a_re[l], ssm_a_im[l], ssm_log_dt[l],
                                              ssm_b_re[l], ssm_b_im[l], ssm_c_re[l], ssm_c_im[l])
    ssm = _s5_mixer(h3, bre, bim, cre, cim, coef,
                    ssm_d[l].reshape(1, D_SSM).astype(F32), ssm_w_glu[l].astype(BF16))

    x1, xn2 = _out_proj(attn.reshape(b * s, D_ATTN), ssm.reshape(b * s, D_SSM),
                        w_out[l].astype(BF16), x2, ln_post_mix[l][None, :], ln_pre_ffn[l][None, :])

    gated = _up_conv(xn2, w_up[l].astype(F32), conv_w[l].reshape(3, 2 * D_FF),
                     conv_b[l].reshape(1, 2 * D_FF), s)
    y = _down_proj(gated, w_down[l].astype(BF16), x1, ln_post_ffn[l][None, :])
    return y.reshape(b, s, d)
```
